```python
import jax
import jax.numpy as jnp
from jax import lax
import numpy as np


D_MODEL = 1024
BATCH = 16
SEQ = 4096
DEPTH = 2
DEC_BATCH = 2
DEC_SEQ = 16384
PAST_LEN = 128

HEAD_DIM = 64
A_HEADS = 8
A_W = A_HEADS * HEAD_DIM
W_RANK = 64
ICL_RANK = 64
G_RANK = 128
DECAY_SCALE = 0.606531
GN_EPS = 64e-5
A_IN = 3 * A_W + W_RANK + ICL_RANK + G_RANK
A_SPLITS = (A_W, 2 * A_W, 3 * A_W, 3 * A_W + W_RANK, 3 * A_W + W_RANK + ICL_RANK)
B_HEADS = 8
B_W = B_HEADS * HEAD_DIM
B_IN = 4 * B_W + 4 * B_HEADS
EVEN_IN = A_IN + B_IN
EVEN_OUT = A_W + B_W
C_HEADS = 8
C_HEAD_DIM = 128
C_W = C_HEADS * C_HEAD_DIM
ODD_IN = 4 * C_W + 4 * C_HEADS
CONV_W = 5
CHUNK = 64
D_FF = 2816
N_EXPERTS = 8
TOP_K = 2
D_FF_EXPERT = 1408
N_EVEN = (DEPTH + 1) // 2
N_ODD = DEPTH // 2

kernel_name = 'bidir_rwkv7_gdn_mlstm_moe_encoder'


def rms_norm(x, g, eps=1e-6):
    xf = x.astype(jnp.float32)
    y = xf * lax.rsqrt(jnp.mean(xf * xf, axis=-1, keepdims=True) + eps)
    return (y * g.astype(jnp.float32)).astype(x.dtype)


def head_rms_norm(x, g, eps=1e-6):
    return x * lax.rsqrt(jnp.mean(x * x, axis=-1, keepdims=True) + eps) * g


def l2_normalize(x, eps=1e-6):
    return x * lax.rsqrt(jnp.sum(x * x, axis=-1, keepdims=True) + eps)


def rev(t):
    return jnp.flip(t, axis=1)


def centred_shift(x):
    xp = jnp.pad(x, ((0, 0), (1, 1), (0, 0)))
    return 0.5 * (xp[:, :-2] + xp[:, 2:])


def centred_dwconv(x, w):
    k = w.shape[0]
    return lax.conv_general_dilated(
        x, w[:, None, :].astype(x.dtype), window_strides=(1,), padding=[(k // 2, k // 2)],
        dimension_numbers=('NWC', 'WIO', 'NWC'), feature_group_count=x.shape[-1])


def to_chunks(t, n_chunks):
    b, _, h = t.shape[:3]
    t = t.reshape((b, n_chunks, CHUNK, h) + t.shape[3:])
    return jnp.moveaxis(t, 3, 1)


def from_chunks(o):
    n, b, h, l, d = o.shape
    return o.transpose(1, 0, 3, 2, 4).reshape(b, n * l, h, d)


def rwkv7_scan(r, w, k, v, a, b, reverse):
    bsz, _, h, n = r.shape

    def step(s, inp):
        r_t, w_t, k_t, v_t, a_t, b_t = inp
        sa = jnp.einsum('bhvk,bhk->bhv', s, a_t)
        s = s * w_t[:, :, None, :] + sa[..., None] * b_t[:, :, None, :] + v_t[..., None] * k_t[:, :, None, :]
        return s, jnp.einsum('bhvk,bhk->bhv', s, r_t)

    s0 = jnp.zeros((bsz, h, n, n), jnp.float32)
    xs = tuple(jnp.moveaxis(t, 1, 0) for t in (r, w, k, v, a, b))
    _, y = lax.scan(step, s0, xs, reverse=reverse)
    return jnp.moveaxis(y, 0, 1)


def rwkv7_mix(pa, shift_mu, w0, w_up, a0, a_up, g_up, k_k, k_a, r_k, ln_g, ln_b):
    bsz, t, _ = pa.shape
    heads = lambda z: z.reshape(bsz, t, A_HEADS, HEAD_DIM)
    pa = pa + shift_mu * (centred_shift(pa) - pa)
    r, k, v, wd, ad, gd = jnp.split(pa, A_SPLITS, axis=-1)
    wd = jnp.tanh(wd)
    gate = jax.nn.sigmoid(gd) @ g_up
    kk = l2_normalize(heads(k * k_k))
    rh, vh = heads(r), heads(v)

    def direction(d, reverse):
        w = jnp.exp(-DECAY_SCALE * jax.nn.sigmoid(w0[d] + wd @ w_up[d]))
        icl = jax.nn.sigmoid(a0[d] + ad @ a_up[d])
        kd = heads(k * (1.0 + (icl - 1.0) * k_a))
        y = rwkv7_scan(rh, heads(w), kd, vh, -kk, kk * heads(icl), reverse)
        return y, kd

    y_f, k_f = direction(0, False)
    y_b, k_b = direction(1, True)
    y = y_f + y_b
    mu = jnp.mean(y, axis=-1, keepdims=True)
    var = jnp.mean(jnp.square(y - mu), axis=-1, keepdims=True)
    y = ((y - mu) * lax.rsqrt(var + GN_EPS)).reshape(bsz, t, A_W) * ln_g + ln_b
    bonus = jnp.sum(rh * (k_f + k_b) * r_k, axis=-1, keepdims=True) * vh
    return (y + bonus.reshape(bsz, t, A_W)) * gate


def gated_delta_chunked(q, k, v, g, beta):
    bsz, t, h, dk = q.shape
    dv = v.shape[-1]
    n = t // CHUNK
    q = to_chunks(q, n) * dk ** -0.5
    k = to_chunks(k, n)
    v = to_chunks(v, n)
    beta = to_chunks(beta, n)
    gc = jnp.cumsum(to_chunks(g, n), axis=-1)
    tril = jnp.tril(jnp.ones((CHUNK, CHUNK), dtype=bool))
    strict = jnp.tril(jnp.ones((CHUNK, CHUNK), dtype=bool), -1)
    diff = gc[..., :, None] - gc[..., None, :]
    decay = jnp.where(tril, jnp.exp(jnp.where(tril, diff, 0.0)), 0.0)
    kb = k * beta[..., None]
    a_mat = jnp.einsum('bhnid,bhnjd->bhnij', kb, k) * decay
    m_mat = jnp.eye(CHUNK, dtype=a_mat.dtype) + jnp.where(strict, a_mat, 0.0)
    rhs = jnp.concatenate([v * beta[..., None], kb * jnp.exp(gc)[..., None]], axis=-1)
    sol = lax.linalg.triangular_solve(m_mat, rhs, left_side=True, lower=True, unit_diagonal=True)
    u, w = sol[..., :dv], sol[..., dv:]
    qk = jnp.einsum('bhnid,bhnjd->bhnij', q, k) * decay
    q_dec = q * jnp.exp(gc)[..., None]
    k_dec = k * jnp.exp(gc[..., -1:] - gc)[..., None]
    g_last = jnp.exp(gc[..., -1])

    def step(s, inp):
        u_c, w_c, qk_c, qd_c, kd_c, gl_c = inp
        v_new = u_c - jnp.einsum('bhlk,bhkv->bhlv', w_c, s)
        o = jnp.einsum('bhlk,bhkv->bhlv', qd_c, s) + jnp.einsum('bhij,bhjv->bhiv', qk_c, v_new)
        s = s * gl_c[..., None, None] + jnp.einsum('bhlk,bhlv->bhkv', kd_c, v_new)
        return s, o

    xs = tuple(jnp.moveaxis(z, 2, 0) for z in (u, w, qk, q_dec, k_dec, g_last))
    s0 = jnp.zeros((bsz, h, dk, dv), jnp.float32)
    _, o = lax.scan(step, s0, xs)
    return from_chunks(o)


def gdn_mix(pb, conv_w, a_log, dt_bias, norm_g):
    bsz, t, _ = pb.shape
    heads = lambda z: z.reshape(bsz, t, B_HEADS, HEAD_DIM)
    qkv = jax.nn.silu(centred_dwconv(pb[..., :3 * B_W], conv_w))
    q = l2_normalize(heads(qkv[..., :B_W]))
    k = l2_normalize(heads(qkv[..., B_W:2 * B_W]))
    v = heads(qkv[..., 2 * B_W:])
    z = pb[..., 3 * B_W:4 * B_W]
    gl = pb[..., 4 * B_W:].reshape(bsz, t, 4, B_HEADS)
    log_decay = -jnp.exp(a_log) * jax.nn.softplus(gl[:, :, :2] + dt_bias)
    beta = jax.nn.sigmoid(gl[:, :, 2:])
    o_f = gated_delta_chunked(q, k, v, log_decay[:, :, 0], beta[:, :, 0])
    o_b = rev(gated_delta_chunked(rev(q), rev(k), rev(v), rev(log_decay[:, :, 1]), rev(beta[:, :, 1])))
    o = head_rms_norm(o_f + o_b, norm_g).reshape(bsz, t, B_W)
    return o * jax.nn.silu(z)


def mlstm_chunked(q, k, v, log_i, log_f):
    bsz, t, h, d = q.shape
    n = t // CHUNK
    q = to_chunks(q, n) * d ** -0.5
    k = to_chunks(k, n)
    v = to_chunks(v, n)
    log_i = to_chunks(log_i, n)
    bcum = jnp.cumsum(to_chunks(log_f, n), axis=-1)
    tril = jnp.tril(jnp.ones((CHUNK, CHUNK), dtype=bool))
    dmat = jnp.where(tril, bcum[..., :, None] - bcum[..., None, :] + log_i[..., None, :], -jnp.inf)
    dmax = jnp.max(dmat, axis=-1)
    w_end = bcum[..., -1:] - bcum + log_i
    w_end_max = jnp.max(w_end, axis=-1)
    qk = jnp.einsum('bhnld,bhnsd->bhnls', q, k)

    def step(carry, inp):
        c_mem, n_mem, m = carry
        q_c, k_c, v_c, qk_c, d_c, dmax_c, b_c, we_c, wem_c = inp
        a_inter = b_c + m[..., None]
        m_row = jnp.maximum(a_inter, dmax_c)
        s_inter = jnp.exp(a_inter - m_row)
        p = qk_c * jnp.exp(d_c - m_row[..., None])
        num = s_inter[..., None] * jnp.einsum('bhld,bhde->bhle', q_c, c_mem) + jnp.einsum('bhls,bhse->bhle', p, v_c)
        den = s_inter * jnp.einsum('bhld,bhd->bhl', q_c, n_mem) + jnp.sum(p, axis=-1)
        h_c = num / jnp.maximum(jnp.abs(den), jnp.exp(-m_row))[..., None]
        m_new = jnp.maximum(b_c[..., -1] + m, wem_c)
        s_old = jnp.exp(b_c[..., -1] + m - m_new)
        s_k = jnp.exp(we_c - m_new[..., None])[..., None] * k_c
        c_mem = s_old[..., None, None] * c_mem + jnp.einsum('bhld,bhle->bhde', s_k, v_c)
        n_mem = s_old[..., None] * n_mem + jnp.sum(s_k, axis=2)
        return (c_mem, n_mem, m_new), h_c

    xs = tuple(jnp.moveaxis(z, 2, 0) for z in (q, k, v, qk, dmat, dmax, bcum, w_end, w_end_max))
    carry0 = (jnp.zeros((bsz, h, d, d), jnp.float32), jnp.zeros((bsz, h, d), jnp.float32),
              jnp.zeros((bsz, h), jnp.float32))
    _, o = lax.scan(step, carry0, xs)
    return from_chunks(o)


def even_mixer(h, p, i):
    proj = (h @ p['even_w_in'][i]).astype(jnp.float32)
    ya = rwkv7_mix(proj[..., :A_IN], p['rwkv_shift_mu'][i], p['rwkv_w0'][i], p['rwkv_w_up'][i],
                   p['rwkv_a0'][i], p['rwkv_a_up'][i], p['rwkv_g_up'][i], p['rwkv_k_k'][i],
                   p['rwkv_k_a'][i], p['rwkv_r_k'][i], p['rwkv_ln_g'][i], p['rwkv_ln_b'][i])
    yb = gdn_mix(proj[..., A_IN:], p['gdn_conv_w'][i], p['gdn_A_log'][i], p['gdn_dt_bias'][i],
                 p['gdn_norm_g'][i])
    y = jnp.concatenate([ya, yb], axis=-1).astype(h.dtype)
    return y @ p['even_w_out'][i]


def odd_mixer(h, p, i):
    bsz, t, _ = h.shape
    heads = lambda z: z.reshape(bsz, t, C_HEADS, C_HEAD_DIM)
    proj = (h @ p['odd_w_in'][i]).astype(jnp.float32)
    qk = jax.nn.silu(centred_dwconv(proj[..., :2 * C_W], p['mlstm_conv_w'][i]))
    q, k = heads(qk[..., :C_W]), heads(qk[..., C_W:])
    v = heads(proj[..., 2 * C_W:3 * C_W])
    o = proj[..., 3 * C_W:4 * C_W]
    gates = (proj[..., 4 * C_W:] + p['mlstm_gate_b'][i]).reshape(bsz, t, 4, C_HEADS)
    log_i = gates[:, :, :2]
    log_f = jax.nn.log_sigmoid(gates[:, :, 2:])
    h_f = mlstm_chunked(q, k, v, log_i[:, :, 0], log_f[:, :, 0])
    h_b = rev(mlstm_chunked(rev(q), rev(k), rev(v), rev(log_i[:, :, 1]), rev(log_f[:, :, 1])))
    hn = head_rms_norm(h_f + h_b, 1.0).reshape(bsz, t, C_W) * p['mlstm_norm_g'][i]
    y = (jax.nn.sigmoid(o) * hn).astype(h.dtype)
    return y @ p['odd_w_out'][i]


def swiglu(x, w1, w3, w2):
    return (jax.nn.silu(x @ w1) * (x @ w3)) @ w2


def moe_swiglu(x, router_w, router_b, w1, w3, w2):
    bsz, t, d = x.shape
    xt = x.reshape(bsz * t, d)
    logits = (xt @ router_w).astype(jnp.float32) + router_b
    top_val, top_idx = lax.top_k(logits, TOP_K)
    top_w = jax.nn.softmax(top_val, axis=-1)
    gates = jnp.sum(jax.nn.one_hot(top_idx, N_EXPERTS, dtype=jnp.float32) * top_w[..., None], axis=1)
    gates = gates.astype(xt.dtype)
    y = jnp.zeros_like(xt)
    for e in range(N_EXPERTS):
        y = y + gates[:, e:e + 1] * swiglu(xt, w1[e], w3[e], w2[e])
    return y.reshape(bsz, t, d)


def encoder_trunk(x, p):
    for layer in range(DEPTH):
        i = layer // 2
        h = rms_norm(x, p['norm_mix_g'][layer])
        if layer % 2 == 0:
            x = x + even_mixer(h, p, i)
            h = rms_norm(x, p['norm_ffn_g'][layer])
            x = x + swiglu(h, p['ffn_w1'][i], p['ffn_w3'][i], p['ffn_w2'][i])
        else:
            x = x + odd_mixer(h, p, i)
            h = rms_norm(x, p['norm_ffn_g'][layer])
            x = x + moe_swiglu(h, p['moe_router_w'][i], p['moe_router_b'][i], p['moe_w1'][i],
                               p['moe_w3'][i], p['moe_w2'][i])
    return rms_norm(x, p['norm_final_g'])


def setup_inputs(seed: int = 0) -> dict:
    key = jax.random.key(seed)
    keys = jax.random.split(key, 40)
    nrm = lambda i, shape, scale: jax.random.normal(keys[i], shape, jnp.float32) * scale
    unif = lambda i, shape, lo, hi: jax.random.uniform(keys[i], shape, jnp.float32, lo, hi)
    D, NE, NO = D_MODEL, N_EVEN, N_ODD
    dt = jnp.exp(unif(19, (NE, 2, B_HEADS), -6.907755, -2.302585))
    i_bias = nrm(24, (NO, 2 * C_HEADS), 0.1)
    f_bias = jnp.tile(jnp.linspace(3.0, 6.0, C_HEADS), (NO, 2)) + nrm(25, (NO, 2 * C_HEADS), 0.1)
    return {
        'x_prompt': nrm(0, (BATCH, SEQ, D), 1.0),
        'x_sample': nrm(1, (DEC_BATCH, DEC_SEQ, D), 1.0),
        'norm_mix_g': 1.0 + nrm(2, (DEPTH, D), 0.02),
        'norm_ffn_g': 1.0 + nrm(3, (DEPTH, D), 0.02),
        'norm_final_g': 1.0 + nrm(4, (D,), 0.02),
        'even_w_in': nrm(5, (NE, D, EVEN_IN), D ** -0.5),
        'rwkv_shift_mu': unif(6, (NE, A_IN), 0.0, 1.0),
        'rwkv_w0': nrm(7, (NE, 2, A_W), 1.0),
        'rwkv_w_up': nrm(8, (NE, 2, W_RANK, A_W), 0.1 * W_RANK ** -0.5),
        'rwkv_a0': nrm(9, (NE, 2, A_W), 0.5),
        'rwkv_a_up': nrm(10, (NE, 2, ICL_RANK, A_W), 0.1 * ICL_RANK ** -0.5),
        'rwkv_g_up': nrm(11, (NE, G_RANK, A_W), G_RANK ** -0.5),
        'rwkv_k_k': 0.85 + nrm(12, (NE, A_W), 0.02),
        'rwkv_k_a': 1.0 + nrm(13, (NE, A_W), 0.02),
        'rwkv_r_k': nrm(14, (NE, A_HEADS, HEAD_DIM), 0.1),
        'rwkv_ln_g': 1.0 + nrm(15, (NE, A_W), 0.02),
        'rwkv_ln_b': nrm(16, (NE, A_W), 0.02),
        'gdn_conv_w': nrm(17, (NE, CONV_W, 3 * B_W), CONV_W ** -0.5),
        'gdn_A_log': jnp.log(unif(18, (NE, 2, B_HEADS), 1.0, 16.0)),
        'gdn_dt_bias': dt + jnp.log(-jnp.expm1(-dt)),
        'gdn_norm_g': 1.0 + nrm(20, (NE, HEAD_DIM), 0.02),
        'even_w_out': nrm(21, (NE, EVEN_OUT, D), EVEN_OUT ** -0.5),
        'odd_w_in': nrm(22, (NO, D, ODD_IN), D ** -0.5),
        'mlstm_conv_w': nrm(23, (NO, CONV_W, 2 * C_W), CONV_W ** -0.5),
        'mlstm_gate_b': jnp.concatenate([i_bias, f_bias], axis=-1),
        'mlstm_norm_g': 1.0 + nrm(26, (NO, C_W), 0.02),
        'odd_w_out': nrm(27, (NO, C_W, D), C_W ** -0.5),
        'ffn_w1': nrm(28, (NE, D, D_FF), D ** -0.5),
        'ffn_w3': nrm(29, (NE, D, D_FF), D ** -0.5),
        'ffn_w2': nrm(30, (NE, D_FF, D), D_FF ** -0.5),
        'moe_router_w': nrm(31, (NO, D, N_EXPERTS), D ** -0.5),
        'moe_router_b': nrm(32, (NO, N_EXPERTS), 0.01),
        'moe_w1': nrm(33, (NO, N_EXPERTS, D, D_FF_EXPERT), D ** -0.5),
        'moe_w3': nrm(34, (NO, N_EXPERTS, D, D_FF_EXPERT), D ** -0.5),
        'moe_w2': nrm(35, (NO, N_EXPERTS, D_FF_EXPERT, D), D_FF_EXPERT ** -0.5),
    }


def reference(x_prompt, x_sample, norm_mix_g, norm_ffn_g, norm_final_g, even_w_in, rwkv_shift_mu,
              rwkv_w0, rwkv_w_up, rwkv_a0, rwkv_a_up, rwkv_g_up, rwkv_k_k, rwkv_k_a, rwkv_r_k,
              rwkv_ln_g, rwkv_ln_b, gdn_conv_w, gdn_A_log, gdn_dt_bias, gdn_norm_g, even_w_out,
              odd_w_in, mlstm_conv_w, mlstm_gate_b, mlstm_norm_g, odd_w_out, ffn_w1, ffn_w3, ffn_w2,
              moe_router_w, moe_router_b, moe_w1, moe_w3, moe_w2):
    params = dict(
        norm_mix_g=norm_mix_g, norm_ffn_g=norm_ffn_g, norm_final_g=norm_final_g,
        even_w_in=even_w_in, rwkv_shift_mu=rwkv_shift_mu, rwkv_w0=rwkv_w0, rwkv_w_up=rwkv_w_up,
        rwkv_a0=rwkv_a0, rwkv_a_up=rwkv_a_up, rwkv_g_up=rwkv_g_up, rwkv_k_k=rwkv_k_k,
        rwkv_k_a=rwkv_k_a, rwkv_r_k=rwkv_r_k, rwkv_ln_g=rwkv_ln_g, rwkv_ln_b=rwkv_ln_b,
        gdn_conv_w=gdn_conv_w, gdn_A_log=gdn_A_log, gdn_dt_bias=gdn_dt_bias, gdn_norm_g=gdn_norm_g,
        even_w_out=even_w_out, odd_w_in=odd_w_in, mlstm_conv_w=mlstm_conv_w,
        mlstm_gate_b=mlstm_gate_b, mlstm_norm_g=mlstm_norm_g, odd_w_out=odd_w_out,
        ffn_w1=ffn_w1, ffn_w3=ffn_w3, ffn_w2=ffn_w2, moe_router_w=moe_router_w,
        moe_router_b=moe_router_b, moe_w1=moe_w1, moe_w3=moe_w3, moe_w2=moe_w2)
    y_prompt = encoder_trunk(x_prompt, params)
    y_sample = encoder_trunk(x_sample, params)
    return (y_prompt, y_sample)
```

```python
import functools

import jax
import jax.numpy as jnp
from jax import lax
from jax.experimental import pallas as pl
from jax.experimental.pallas import tpu as pltpu

F32 = jnp.float32
BF16 = jnp.bfloat16

D_MODEL = 1024
HEAD_DIM = 64
A_HEADS = 8
A_W = 512
W_RANK = 64
ICL_RANK = 64
G_RANK = 128
DECAY_SCALE = 0.606531
GN_EPS = 64e-5
A_IN = 1792
B_HEADS = 8
B_W = 512
B_IN_PAD = 2176
C_HEADS = 8
C_HEAD_DIM = 128
C_W = 1024
C_IN_PAD = 4224
CONV_W = 5
CHUNK = 64
D_FF = 2816
N_EXPERTS = 8
D_FF_EXPERT = 1408
HALO = 8
LANES = 128

VMEM_LIMIT = 56 * 1024 * 1024

NT_DIMS = (((1,), (1,)), ((), ()))
TN_DIMS = (((0,), (0,)), ((), ()))


def _dot(a, b):
    return jnp.dot(a.astype(BF16), b.astype(BF16), preferred_element_type=F32)


def _dot_nt(a, b):
    return lax.dot_general(a.astype(BF16), b.astype(BF16), NT_DIMS, preferred_element_type=F32)


def _dot_tn(a, b):
    return lax.dot_general(a.astype(BF16), b.astype(BF16), TN_DIMS, preferred_element_type=F32)


def _split3(x):
    h = x.astype(BF16)
    r = x - h.astype(F32)
    m = r.astype(BF16)
    l = (r - m.astype(F32)).astype(BF16)
    return h, m, l


def _dot_exact_rhs(x, c):
    h, m, l = _split3(x)
    return (jnp.dot(h, c, preferred_element_type=F32) + jnp.dot(m, c, preferred_element_type=F32)
            + jnp.dot(l, c, preferred_element_type=F32))


def _dot_exact_lhs(c, x):
    h, m, l = _split3(x)
    return (jnp.dot(c, h, preferred_element_type=F32) + jnp.dot(c, m, preferred_element_type=F32)
            + jnp.dot(c, l, preferred_element_type=F32))


def _dot_tn_exact_rhs(x, c):
    h, m, l = _split3(x)
    f = lambda p: lax.dot_general(p, c, TN_DIMS, preferred_element_type=F32)
    return f(h) + f(m) + f(l)


def _dot3(a, b):
    ah = a.astype(BF16)
    al = (a - ah.astype(F32)).astype(BF16)
    bh = b.astype(BF16)
    bl = (b - bh.astype(F32)).astype(BF16)
    f = lambda p, q: jnp.dot(p, q, preferred_element_type=F32)
    return f(ah, bh) + f(ah, bl) + f(al, bh)


def _sigmoid(x):
    return 1.0 / (1.0 + jnp.exp(-x))


def _silu(x):
    return x * _sigmoid(x)


def _softplus(x):
    return jnp.maximum(x, 0.0) + jnp.log(1.0 + jnp.exp(-jnp.abs(x)))


def _tri_inverse(n_mat, eye):
    t = eye + n_mat
    p = n_mat
    for _ in range(5):
        p = _dot3(p, p)
        t = t + _dot3(t, p)
    return t


def _masks(reverse):
    row = lax.broadcasted_iota(jnp.int32, (CHUNK, CHUNK), 0)
    col = lax.broadcasted_iota(jnp.int32, (CHUNK, CHUNK), 1)
    strict = (col > row) if reverse else (col < row)
    incl = (col >= row) if reverse else (col <= row)
    eye = (col == row).astype(F32)
    return strict, incl, eye


def _norm_matmul_kernel(x_ref, g_ref, w_ref, *out_refs, widths, tn):
    x = x_ref[...]
    h = (x * lax.rsqrt(jnp.mean(x * x, axis=-1, keepdims=True) + 1e-6) * g_ref[...]).astype(BF16)
    off = 0
    for o_ref, wd in zip(out_refs, widths):
        for c in range(0, wd, tn):
            cw = min(tn, wd - c)
            o_ref[:, c:c + cw] = jnp.dot(h, w_ref[:, off + c:off + c + cw], preferred_element_type=F32)
        off += wd


def norm_matmul(x2d, g, w, widths, tm=256, tn=512):
    n, d = x2d.shape
    ntot = w.shape[1]
    assert sum(widths) == ntot and n % tm == 0
    return pl.pallas_call(
        functools.partial(_norm_matmul_kernel, widths=widths, tn=tn),
        out_shape=[jax.ShapeDtypeStruct((n, wd), F32) for wd in widths],
        grid=(n // tm,),
        in_specs=[pl.BlockSpec((tm, d), lambda i: (i, 0)),
                  pl.BlockSpec((1, d), lambda i: (0, 0)),
                  pl.BlockSpec((d, ntot), lambda i: (0, 0))],
        out_specs=[pl.BlockSpec((tm, wd), lambda i: (i, 0)) for wd in widths],
        compiler_params=pltpu.CompilerParams(dimension_semantics=("parallel",), vmem_limit_bytes=VMEM_LIMIT),
        name="norm_matmul",
    )(x2d, g.reshape(1, d), w)


def _matmul_res_kernel(*refs, n_in):
    y_refs = refs[:n_in]
    w_refs = refs[n_in:2 * n_in]
    x_ref, o_ref = refs[2 * n_in], refs[2 * n_in + 1]
    acc = x_ref[...]
    for y_ref, w_ref in zip(y_refs, w_refs):
        acc = acc + jnp.dot(y_ref[...], w_ref[...], preferred_element_type=F32)
    o_ref[...] = acc


def matmul_residual(ys, ws, x2d, tm=512):
    n, d = x2d.shape
    assert n % tm == 0
    return pl.pallas_call(
        functools.partial(_matmul_res_kernel, n_in=len(ys)),
        out_shape=jax.ShapeDtypeStruct((n, d), F32),
        grid=(n // tm,),
        in_specs=([pl.BlockSpec((tm, y.shape[1]), lambda i: (i, 0)) for y in ys]
                  + [pl.BlockSpec(w.shape, lambda i: (0, 0)) for w in ws]
                  + [pl.BlockSpec((tm, d), lambda i: (i, 0))]),
        out_specs=pl.BlockSpec((tm, d), lambda i: (i, 0)),
        compiler_params=pltpu.CompilerParams(dimension_semantics=("parallel",), vmem_limit_bytes=VMEM_LIMIT),
        name="matmul_residual",
    )(*ys, *ws, x2d)


def _ffn_kernel(x_ref, g_ref, w1_ref, w3_ref, w2_ref, o_ref, *, tf):
    x = x_ref[...]
    h = (x * lax.rsqrt(jnp.mean(x * x, axis=-1, keepdims=True) + 1e-6) * g_ref[...]).astype(BF16)
    acc = x
    for c in range(0, D_FF, tf):
        a = jnp.dot(h, w1_ref[:, c:c + tf], preferred_element_type=F32)
        b = jnp.dot(h, w3_ref[:, c:c + tf], preferred_element_type=F32)
        acc = acc + jnp.dot((_silu(a) * b).astype(BF16), w2_ref[c:c + tf, :], preferred_element_type=F32)
    o_ref[...] = acc


def ffn_residual(x2d, g, w1, w3, w2, tm=512, tf=256):
    n, d = x2d.shape
    assert n % tm == 0 and D_FF % tf == 0
    const = lambda shape: pl.BlockSpec(shape, lambda i: (0, 0), pipeline_mode=pl.Buffered(1))
    return pl.pallas_call(
        functools.partial(_ffn_kernel, tf=tf),
        out_shape=jax.ShapeDtypeStruct((n, d), F32),
        grid=(n // tm,),
        in_specs=[pl.BlockSpec((tm, d), lambda i: (i, 0)), const((1, d)),
                  const(w1.shape), const(w3.shape), const(w2.shape)],
        out_specs=pl.BlockSpec((tm, d), lambda i: (i, 0)),
        compiler_params=pltpu.CompilerParams(dimension_semantics=("parallel",), vmem_limit_bytes=VMEM_LIMIT),
        name="ffn_residual",
    )(x2d, g.reshape(1, d), w1, w3, w2)


def _moe_kernel(x_ref, g_ref, rw_ref, rb_ref, w1_ref, w3_ref, w2_ref, gf_ref, o_ref, h_ref, gate_ref, acc_ref, *, tf):
    e = pl.program_id(1)

    @pl.when(e == 0)
    def _():
        x = x_ref[...]
        hf = x * lax.rsqrt(jnp.mean(x * x, axis=-1, keepdims=True) + 1e-6) * g_ref[...]
        h_ref[...] = hf.astype(BF16)
        logits = _dot3(hf, rw_ref[...]) + rb_ref[...]
        lane = lax.broadcasted_iota(jnp.int32, logits.shape, 1)
        m1 = jnp.max(logits, axis=-1, keepdims=True)
        i1 = jnp.min(jnp.where(logits == m1, lane, LANES), axis=-1, keepdims=True)
        rest = jnp.where(lane == i1, -jnp.inf, logits)
        m2 = jnp.max(rest, axis=-1, keepdims=True)
        i2 = jnp.min(jnp.where(rest == m2, lane, LANES), axis=-1, keepdims=True)
        e2 = jnp.exp(m2 - m1)
        den = 1.0 + e2
        gate_ref[...] = jnp.where(lane == i1, 1.0 / den, jnp.where(lane == i2, e2 / den, 0.0))
        acc_ref[...] = x

    lane = lax.broadcasted_iota(jnp.int32, gate_ref.shape, 1)
    ge = jnp.sum(jnp.where(lane == e, gate_ref[...], 0.0), axis=-1, keepdims=True)
    h = h_ref[...]
    y = jnp.zeros(acc_ref.shape, F32)
    for c in range(0, D_FF_EXPERT, tf):
        a = jnp.dot(h, w1_ref[:, c:c + tf], preferred_element_type=F32)
        b = jnp.dot(h, w3_ref[:, c:c + tf], preferred_element_type=F32)
        y = y + jnp.dot((_silu(a) * b).astype(BF16), w2_ref[c:c + tf, :], preferred_element_type=F32)
    acc_ref[...] += ge * y

    @pl.when(e == N_EXPERTS - 1)
    def _():
        z = acc_ref[...]
        o_ref[...] = z * lax.rsqrt(jnp.mean(z * z, axis=-1, keepdims=True) + 1e-6) * gf_ref[...]


def moe_residual_final(x2d, g, router_w, router_b, w1, w3, w2, g_final, tm=1024, tf=128):
    n, d = x2d.shape
    assert n % tm == 0 and D_FF_EXPERT % tf == 0
    rw = jnp.zeros((d, LANES), F32).at[:, :N_EXPERTS].set(router_w)
    rb = jnp.full((1, LANES), -1e30, F32).at[0, :N_EXPERTS].set(router_b)
    return pl.pallas_call(
        functools.partial(_moe_kernel, tf=tf),
        out_shape=jax.ShapeDtypeStruct((n, d), F32),
        grid=(n // tm, N_EXPERTS),
        in_specs=[pl.BlockSpec((tm, d), lambda i, e: (i, 0)),
                  pl.BlockSpec((1, d), lambda i, e: (0, 0)),
                  pl.BlockSpec((d, LANES), lambda i, e: (0, 0)),
                  pl.BlockSpec((1, LANES), lambda i, e: (0, 0)),
                  pl.BlockSpec((None, d, D_FF_EXPERT), lambda i, e: (e, 0, 0)),
                  pl.BlockSpec((None, d, D_FF_EXPERT), lambda i, e: (e, 0, 0)),
                  pl.BlockSpec((None, D_FF_EXPERT, d), lambda i, e: (e, 0, 0)),
                  pl.BlockSpec((1, d), lambda i, e: (0, 0))],
        out_specs=pl.BlockSpec((tm, d), lambda i, e: (i, 0)),
        scratch_shapes=[pltpu.VMEM((tm, d), BF16), pltpu.VMEM((tm, LANES), F32), pltpu.VMEM((tm, d), F32)],
        compiler_params=pltpu.CompilerParams(dimension_semantics=("parallel", "arbitrary"),
                                             vmem_limit_bytes=VMEM_LIMIT),
        name="moe_residual_final",
    )(x2d, g.reshape(1, d), rw, rb, w1, w3, w2, g_final.reshape(1, d))


def _seq_specs(tb, nt, width, reverse, col_block=0):
    hb = tb // HALO
    nhb = nt * hb
    pos = (lambda t: nt - 1 - t) if reverse else (lambda t: t)
    main = pl.BlockSpec((None, tb, width), lambda b, t: (b, pos(t), col_block))
    prev = pl.BlockSpec((None, HALO, width), lambda b, t: (b, jnp.maximum(pos(t) * hb - 1, 0), col_block))
    nxt = pl.BlockSpec((None, HALO, width), lambda b, t: (b, jnp.minimum((pos(t) + 1) * hb, nhb - 1), col_block))
    return main, prev, nxt


def _fill_extended(xe_ref, x_ref, xp_ref, xn_ref, pos, nt, tb, width=None):
    sl = slice(None) if width is None else slice(0, width)
    xe_ref[0:HALO, :] = jnp.where(pos == 0, 0.0, xp_ref[:, sl])
    xe_ref[HALO:HALO + tb, :] = x_ref[:, sl]
    xe_ref[HALO + tb:2 * HALO + tb, :] = jnp.where(pos == nt - 1, 0.0, xn_ref[:, sl])


def _seg_mats():
    r = lax.broadcasted_iota(jnp.int32, (A_W, A_W), 0) // HEAD_DIM
    c = lax.broadcasted_iota(jnp.int32, (A_W, A_W), 1) // HEAD_DIM
    return (r == c).astype(BF16)


def _rwkv_kernel(*refs, reverse, final, tb, nt):
    if final:
        (x_ref, xp_ref, xn_ref, mu_ref, kk_ref, ka_ref, w0_ref, wup_ref, a0_ref, aup_ref, seg_ref, tri_ref,
         gup_ref, rk_ref, lng_ref, lnb_ref, yf_ref, o_ref,
         xe_ref, s_ref, r_s, k_s, v_s, a_s, b_s, lw_s, y_s) = refs
    else:
        (x_ref, xp_ref, xn_ref, mu_ref, kk_ref, ka_ref, w0_ref, wup_ref, a0_ref, aup_ref, seg_ref, tri_ref,
         o_ref, xe_ref, s_ref, r_s, k_s, v_s, a_s, b_s, lw_s, y_s) = refs
    d = 1 if reverse else 0
    t = pl.program_id(1)
    pos = (nt - 1 - t) if reverse else t
    nc = tb // CHUNK

    @pl.when(t == 0)
    def _():
        s_ref[...] = jnp.zeros(s_ref.shape, F32)

    _fill_extended(xe_ref, x_ref, xp_ref, xn_ref, pos, nt, tb)
    x = x_ref[...]
    nb = 0.5 * (xe_ref[pl.ds(HALO - 1, tb), :] + xe_ref[pl.ds(HALO + 1, tb), :])
    pa = x + mu_ref[...] * (nb - x)
    r = pa[:, 0:A_W]
    k = pa[:, A_W:2 * A_W]
    v = pa[:, 2 * A_W:3 * A_W]
    wd = jnp.tanh(pa[:, 3 * A_W:3 * A_W + W_RANK])
    ad = pa[:, 3 * A_W + W_RANK:3 * A_W + W_RANK + ICL_RANK]
    seg = seg_ref[...]
    kq = k * kk_ref[...]
    kk = kq * lax.rsqrt(_dot_exact_rhs(kq * kq, seg) + 1e-6)
    lw = -DECAY_SCALE * _sigmoid(w0_ref[d:d + 1, :] + _dot(wd, wup_ref[d]))
    icl = _sigmoid(a0_ref[d:d + 1, :] + _dot(ad, aup_ref[d]))
    ka = ka_ref[...]
    r_s[...] = r
    k_s[...] = k * (1.0 + (icl - 1.0) * ka)
    v_s[...] = v
    a_s[...] = -kk
    b_s[...] = kk * icl
    lw_s[...] = lw

    strict, incl, eye = _masks(reverse)
    tri = tri_ref[...]

    def chunk_body(i, carry):
        c = (nc - 1 - i) if reverse else i
        rows = pl.ds(pl.multiple_of(c * CHUNK, CHUNK), CHUNK)
        lwc = lw_s[rows, :]
        cum = _dot_exact_lhs(tri, lwc)
        last = cum[0:1, :] if reverse else cum[CHUNK - 1:CHUNK, :]
        rt = r_s[rows, :] * jnp.exp(cum)
        at = a_s[rows, :] * jnp.exp(cum - lwc)
        einv = jnp.exp(-cum)
        bc = b_s[rows, :]
        kc = k_s[rows, :]
        bt = bc * einv
        kt = kc * einv
        eend = jnp.exp(last - cum)
        bh = bc * eend
        kh = kc * eend
        gl = jnp.exp(last)
        vc = v_s[rows, :]
        ys = []
        for h in range(A_HEADS):
            sl = slice(h * HEAD_DIM, (h + 1) * HEAD_DIM)
            xm = jnp.concatenate([at[:, sl], rt[:, sl]], axis=0)
            ym = jnp.concatenate([bt[:, sl], kt[:, sl]], axis=0)
            p = _dot_nt(xm, ym)
            a_ab = jnp.where(strict, p[0:CHUNK, 0:CHUNK], 0.0)
            a_ak = jnp.where(strict, p[0:CHUNK, CHUNK:], 0.0)
            a_rb = jnp.where(incl, p[CHUNK:, 0:CHUNK], 0.0)
            a_rk = jnp.where(incl, p[CHUNK:, CHUNK:], 0.0)
            tinv = _tri_inverse(a_ab, eye)
            vh = vc[:, sl]
            u0 = _dot3(tinv, _dot(a_ak, vh))
            ahat = _dot3(tinv, at[:, sl])
            s = s_ref[h]
            u = _dot_nt(ahat, s) + u0
            ys.append(_dot_nt(rt[:, sl], s) + _dot(a_rb, u) + _dot(a_rk, vh))
            s_ref[h] = s * gl[:, sl] + _dot_tn(u, bh[:, sl]) + _dot_tn(vh, kh[:, sl])
        y_s[rows, :] = jnp.concatenate(ys, axis=1)
        return carry

    lax.fori_loop(0, nc, chunk_body, 0)

    if not final:
        o_ref[...] = y_s[...]
    else:
        y = yf_ref[...] + y_s[...]
        seg_mean = lambda z: _dot_exact_rhs(z, seg) * (1.0 / HEAD_DIM)
        mu = seg_mean(y)
        yc = y - mu
        var = seg_mean(yc * yc)
        yn = yc * lax.rsqrt(var + GN_EPS) * lng_ref[...] + lnb_ref[...]
        o = 1 - d
        icl_o = _sigmoid(a0_ref[o:o + 1, :] + _dot(ad, aup_ref[o]))
        ksum = k * (2.0 + (icl + icl_o - 2.0) * ka)
        bonus = _dot_exact_rhs(r * ksum * rk_ref[...], seg) * v
        gate = _dot(_sigmoid(pa[:, 3 * A_W + W_RANK + ICL_RANK:]), gup_ref[...])
        o_ref[...] = ((yn + bonus) * gate).astype(o_ref.dtype)


def _tri_const(reverse):
    r = jnp.arange(CHUNK)[:, None]
    c = jnp.arange(CHUNK)[None, :]
    return ((c >= r) if reverse else (c <= r)).astype(BF16)


def rwkv_pass(pa, p, y_fwd, *, reverse, tb):
    bsz, t_len, _ = pa.shape
    nt = t_len // tb
    final = y_fwd is not None
    main, prev, nxt = _seq_specs(tb, nt, A_IN, reverse)
    const2 = lambda a: pl.BlockSpec(a.shape, lambda b, t: (0,) * a.ndim)
    row = lambda a: a.reshape(1, -1).astype(F32)
    consts = [row(p['shift_mu']), row(p['k_k']), row(p['k_a']), p['w0'].astype(F32), p['w_up'].astype(BF16),
              p['a0'].astype(F32), p['a_up'].astype(BF16), _seg_mats(), _tri_const(reverse)]
    args = [pa, pa, pa] + consts
    in_specs = [main, prev, nxt] + [const2(a) for a in consts]
    pos = (lambda t: nt - 1 - t) if reverse else (lambda t: t)
    io_spec = pl.BlockSpec((None, tb, A_W), lambda b, t: (b, pos(t), 0))
    if final:
        extra = [p['g_up'].astype(BF16), row(p['r_k']), row(p['ln_g']), row(p['ln_b'])]
        args += extra + [y_fwd]
        in_specs += [const2(a) for a in extra] + [io_spec]
    blk = lambda: pltpu.VMEM((tb, A_W), F32)
    return pl.pallas_call(
        functools.partial(_rwkv_kernel, reverse=reverse, final=final, tb=tb, nt=nt),
        out_shape=jax.ShapeDtypeStruct((bsz, t_len, A_W), BF16 if final else F32),
        grid=(bsz, nt),
        in_specs=in_specs,
        out_specs=io_spec,
        scratch_shapes=[pltpu.VMEM((tb + 2 * HALO, A_IN), F32), pltpu.VMEM((A_HEADS, HEAD_DIM, HEAD_DIM), F32),
                        blk(), blk(), blk(), blk(), blk(), blk(), blk()],
        compiler_params=pltpu.CompilerParams(dimension_semantics=("parallel", "arbitrary"),
                                             vmem_limit_bytes=VMEM_LIMIT),
        name="rwkv_bwd_final" if final else "rwkv_fwd",
    )(*args)


def rwkv_mix(pa, p, tb):
    y_f = rwkv_pass(pa, p, None, reverse=False, tb=tb)
    return rwkv_pass(pa, p, y_f, reverse=True, tb=tb)


def _gdn_kernel(*refs, reverse, final, tb, nt):
    if final:
        (x_ref, xp_ref, xn_ref, cw_ref, alog_ref, dtb_ref, seg_ref, tri_ref, trit_ref, exg_ref, exb_ref,
         ng_ref, of_ref, o_ref, xe_ref, s_ref, q_s, k_s, v_s, g_s, bx_s, y_s) = refs
    else:
        (x_ref, xp_ref, xn_ref, cw_ref, alog_ref, dtb_ref, seg_ref, tri_ref, trit_ref, exg_ref, exb_ref,
         o_ref, xe_ref, s_ref, q_s, k_s, v_s, g_s, bx_s, y_s) = refs
    d = 1 if reverse else 0
    t = pl.program_id(1)
    pos = (nt - 1 - t) if reverse else t
    nc = tb // CHUNK
    qkv_w = 3 * B_W

    @pl.when(t == 0)
    def _():
        s_ref[...] = jnp.zeros(s_ref.shape, F32)

    _fill_extended(xe_ref, x_ref, xp_ref, xn_ref, pos, nt, tb, width=qkv_w)
    conv = xe_ref[pl.ds(HALO - 2, tb), :] * cw_ref[0:1, :]
    for j in range(1, CONV_W):
        conv = conv + xe_ref[pl.ds(HALO - 2 + j, tb), :] * cw_ref[j:j + 1, :]
    qkv = _silu(conv)
    seg = seg_ref[...]
    q = qkv[:, 0:B_W]
    k = qkv[:, B_W:2 * B_W]
    q_s[...] = q * lax.rsqrt(_dot_exact_rhs(q * q, seg) + 1e-6) * (HEAD_DIM ** -0.5)
    k_s[...] = k * lax.rsqrt(_dot_exact_rhs(k * k, seg) + 1e-6)
    v_s[...] = qkv[:, 2 * B_W:]
    gl = x_ref[:, 4 * B_W:4 * B_W + LANES]
    g_s[...] = -jnp.exp(alog_ref[...]) * _softplus(gl + dtb_ref[...])
    bx_s[...] = _dot_exact_rhs(_sigmoid(gl), exb_ref[...])

    strict, incl, eye = _masks(reverse)
    tri = tri_ref[...]
    trit = trit_ref[...]
    exg = exg_ref[...]

    def chunk_body(i, carry):
        c = (nc - 1 - i) if reverse else i
        rows = pl.ds(pl.multiple_of(c * CHUNK, CHUNK), CHUNK)
        gch = g_s[rows, :]
        gc = _dot_exact_lhs(tri, gch)
        gct = _dot_tn_exact_rhs(gch, trit)
        last = gc[0:1, :] if reverse else gc[CHUNK - 1:CHUNK, :]
        eg = _dot_exact_rhs(jnp.exp(gc), exg)
        eend = _dot_exact_rhs(jnp.exp(last - gc), exg)
        glast = _dot_exact_rhs(jnp.exp(last), exg)
        qc = q_s[rows, :]
        kc = k_s[rows, :]
        vc = v_s[rows, :]
        beta = bx_s[rows, :]
        kb = kc * beta
        vb = vc * beta
        kbe = kb * eg
        qd = qc * eg
        kd = kc * eend
        ys = []
        for h in range(B_HEADS):
            sl = slice(h * HEAD_DIM, (h + 1) * HEAD_DIM)
            lane = d * B_HEADS + h
            diff = gc[:, lane:lane + 1] - gct[lane:lane + 1, :]
            decay = jnp.exp(jnp.where(incl, diff, 0.0))
            kh = kc[:, sl]
            xm = jnp.concatenate([kb[:, sl], qc[:, sl]], axis=0)
            pm = _dot_nt(xm, kh)
            a_mat = jnp.where(strict, pm[0:CHUNK] * decay, 0.0)
            qk = jnp.where(incl, pm[CHUNK:] * decay, 0.0)
            tinv = _tri_inverse(-a_mat, eye)
            rhs = jnp.concatenate([vb[:, sl], kbe[:, sl]], axis=1)
            sol = _dot3(tinv, rhs)
            u = sol[:, 0:HEAD_DIM]
            w = sol[:, HEAD_DIM:]
            s = s_ref[h]
            ws = _dot(jnp.concatenate([w, qd[:, sl]], axis=0), s)
            v_new = u - ws[0:CHUNK]
            ys.append(ws[CHUNK:] + _dot(qk, v_new))
            s_ref[h] = s * glast[:, sl] + _dot_tn(kd[:, sl], v_new)
        y_s[rows, :] = jnp.concatenate(ys, axis=1)
        return carry

    lax.fori_loop(0, nc, chunk_body, 0)

    if not final:
        o_ref[...] = y_s[...]
    else:
        o = of_ref[...] + y_s[...]
        ms = _dot_exact_rhs(o * o, seg) * (1.0 / HEAD_DIM)
        z = x_ref[:, 3 * B_W:4 * B_W]
        o_ref[...] = (o * lax.rsqrt(ms + 1e-6) * ng_ref[...] * _silu(z)).astype(o_ref.dtype)


def _expand_mat(first_lane):
    r = jnp.arange(LANES)[:, None]
    c = jnp.arange(B_W)[None, :] // HEAD_DIM
    return (r == c + first_lane).astype(BF16)


def gdn_pass(pb, p, o_fwd, *, reverse, tb):
    bsz, t_len, _ = pb.shape
    nt = t_len // tb
    final = o_fwd is not None
    d = 1 if reverse else 0
    main, prev, nxt = _seq_specs(tb, nt, B_IN_PAD, reverse)
    const2 = lambda a: pl.BlockSpec(a.shape, lambda b, t: (0,) * a.ndim)
    pad_row = lambda a: jnp.zeros((1, LANES), F32).at[0, :a.size].set(a.reshape(-1))
    consts = [p['conv_w'].astype(F32), pad_row(p['A_log']), pad_row(p['dt_bias']), _seg_mats(),
              _tri_const(reverse), _tri_const(not reverse), _expand_mat(d * B_HEADS),
              _expand_mat(2 * B_HEADS + d * B_HEADS)]
    args = [pb, pb, pb] + consts
    in_specs = [main, prev, nxt] + [const2(a) for a in consts]
    pos = (lambda t: nt - 1 - t) if reverse else (lambda t: t)
    io_spec = pl.BlockSpec((None, tb, B_W), lambda b, t: (b, pos(t), 0))
    if final:
        ng = jnp.tile(p['norm_g'].astype(F32), B_HEADS).reshape(1, B_W)
        args += [ng, o_fwd]
        in_specs += [const2(ng), io_spec]
    blk = lambda: pltpu.VMEM((tb, B_W), F32)
    return pl.pallas_call(
        functools.partial(_gdn_kernel, reverse=reverse, final=final, tb=tb, nt=nt),
        out_shape=jax.ShapeDtypeStruct((bsz, t_len, B_W), BF16 if final else F32),
        grid=(bsz, nt),
        in_specs=in_specs,
        out_specs=io_spec,
        scratch_shapes=[pltpu.VMEM((tb + 2 * HALO, 3 * B_W), F32), pltpu.VMEM((B_HEADS, HEAD_DIM, HEAD_DIM), F32),
                        blk(), blk(), blk(), pltpu.VMEM((tb, LANES), F32), blk(), blk()],
        compiler_params=pltpu.CompilerParams(dimension_semantics=("parallel", "arbitrary"),
                                             vmem_limit_bytes=VMEM_LIMIT),
        name="gdn_bwd_final" if final else "gdn_fwd",
    )(*args)


def gdn_mix(pb, p, tb):
    o_f = gdn_pass(pb, p, None, reverse=False, tb=tb)
    return gdn_pass(pb, p, o_f, reverse=True, tb=tb)


def _mlstm_kernel(*refs, reverse, final, tb, nt):
    if final:
        (x_ref, xp_ref, xn_ref, cw_ref, gb_ref, tri_ref, trit_ref, ng_ref, hf_ref, o_ref,
         xe_ref, c_ref, n_ref, m_ref, q_s, k_s, li_s, lf_s, y_s) = refs
    else:
        (x_ref, xp_ref, xn_ref, cw_ref, gb_ref, tri_ref, trit_ref, o_ref,
         xe_ref, c_ref, n_ref, m_ref, q_s, k_s, li_s, lf_s, y_s) = refs
    d = 1 if reverse else 0
    t = pl.program_id(1)
    pos = (nt - 1 - t) if reverse else t
    nc = tb // CHUNK
    qk_w = 2 * C_W
    dh = C_HEAD_DIM

    @pl.when(t == 0)
    def _():
        c_ref[...] = jnp.zeros(c_ref.shape, F32)
        n_ref[...] = jnp.zeros(n_ref.shape, F32)
        m_ref[...] = jnp.zeros(m_ref.shape, F32)

    _fill_extended(xe_ref, x_ref, xp_ref, xn_ref, pos, nt, tb, width=qk_w)
    conv = xe_ref[pl.ds(HALO - 2, tb), :] * cw_ref[0:1, :]
    for j in range(1, CONV_W):
        conv = conv + xe_ref[pl.ds(HALO - 2 + j, tb), :] * cw_ref[j:j + 1, :]
    qk = _silu(conv)
    q_s[...] = qk[:, 0:C_W] * (dh ** -0.5)
    k_s[...] = qk[:, C_W:]
    gates = x_ref[:, 4 * C_W:4 * C_W + LANES] + gb_ref[...]
    li_s[...] = gates
    lf_s[...] = -_softplus(-gates)

    strict, incl, eye = _masks(reverse)
    tri = tri_ref[...]
    trit = trit_ref[...]
    eye_b = eye.astype(BF16)

    def chunk_body(i, carry):
        c = (nc - 1 - i) if reverse else i
        rows = pl.ds(pl.multiple_of(c * CHUNK, CHUNK), CHUNK)
        lfc = lf_s[rows, :]
        lic = li_s[rows, :]
        bcum = _dot_exact_lhs(tri, lfc)
        bcum_t = _dot_tn_exact_rhs(lfc, trit)
        li_t = _dot_tn_exact_rhs(lic, eye_b)
        last = bcum[0:1, :] if reverse else bcum[CHUNK - 1:CHUNK, :]
        qc = q_s[rows, :]
        kc = k_s[rows, :]
        vc = x_ref[rows, 2 * C_W:3 * C_W]
        ys = []
        for h in range(C_HEADS):
            sl = slice(h * dh, (h + 1) * dh)
            il = d * C_HEADS + h
            fl = 2 * C_HEADS + d * C_HEADS + h
            b_col = bcum[:, fl:fl + 1]
            dmat = jnp.where(incl, b_col - bcum_t[fl:fl + 1, :] + li_t[il:il + 1, :], -jnp.inf)
            dmax = jnp.max(dmat, axis=-1, keepdims=True)
            b_last = last[:, fl:fl + 1]
            w_end = b_last - b_col + lic[:, il:il + 1]
            wem = jnp.max(w_end, axis=0, keepdims=True)
            qh, kh, vh = qc[:, sl], kc[:, sl], vc[:, sl]
            m = m_ref[h]
            cm = c_ref[h]
            nm = n_ref[h]
            a_inter = b_col + m
            m_row = jnp.maximum(a_inter, dmax)
            s_inter = jnp.exp(a_inter - m_row)
            pm = _dot_nt(qh, kh) * jnp.exp(dmat - m_row)
            num = s_inter * _dot(qh, cm) + _dot(pm, vh)
            den = s_inter * jnp.sum(qh * nm, axis=-1, keepdims=True) + jnp.sum(pm, axis=-1, keepdims=True)
            ys.append(num / jnp.maximum(jnp.abs(den), jnp.exp(-m_row)))
            m_new = jnp.maximum(b_last + m, wem)
            s_old = jnp.exp(b_last + m - m_new)
            s_k = jnp.exp(w_end - m_new) * kh
            c_ref[h] = s_old * cm + _dot_tn(s_k, vh)
            n_ref[h] = s_old * nm + jnp.sum(s_k, axis=0, keepdims=True)
            m_ref[h] = m_new
        y_s[rows, :] = jnp.concatenate(ys, axis=1)
        return carry

    lax.fori_loop(0, nc, chunk_body, 0)

    if not final:
        o_ref[...] = y_s[...]
    else:
        hs = hf_ref[...] + y_s[...]
        outs = []
        for h in range(C_HEADS):
            sl = slice(h * dh, (h + 1) * dh)
            hh = hs[:, sl]
            outs.append(hh * lax.rsqrt(jnp.mean(hh * hh, axis=-1, keepdims=True) + 1e-6))
        hn = jnp.concatenate(outs, axis=1) * ng_ref[...]
        og = x_ref[:, 3 * C_W:4 * C_W]
        o_ref[...] = (_sigmoid(og) * hn).astype(o_ref.dtype)


def mlstm_pass(pc, p, h_fwd, *, reverse, tb):
    bsz, t_len, _ = pc.shape
    nt = t_len // tb
    final = h_fwd is not None
    main, prev, nxt = _seq_specs(tb, nt, C_IN_PAD, reverse)
    const2 = lambda a: pl.BlockSpec(a.shape, lambda b, t: (0,) * a.ndim)
    gb = jnp.zeros((1, LANES), F32).at[0, :4 * C_HEADS].set(p['gate_b'].astype(F32))
    consts = [p['conv_w'].astype(F32), gb, _tri_const(reverse), _tri_const(not reverse)]
    args = [pc, pc, pc] + consts
    in_specs = [main, prev, nxt] + [const2(a) for a in consts]
    pos = (lambda t: nt - 1 - t) if reverse else (lambda t: t)
    io_spec = pl.BlockSpec((None, tb, C_W), lambda b, t: (b, pos(t), 0))
    if final:
        ng = p['norm_g'].astype(F32).reshape(1, C_W)
        args += [ng, h_fwd]
        in_specs += [const2(ng), io_spec]
    blk = lambda: pltpu.VMEM((tb, C_W), F32)
    gate_blk = lambda: pltpu.VMEM((tb, LANES), F32)
    return pl.pallas_call(
        functools.partial(_mlstm_kernel, reverse=reverse, final=final, tb=tb, nt=nt),
        out_shape=jax.ShapeDtypeStruct((bsz, t_len, C_W), BF16 if final else F32),
        grid=(bsz, nt),
        in_specs=in_specs,
        out_specs=io_spec,
        scratch_shapes=[pltpu.VMEM((tb + 2 * HALO, 2 * C_W), F32),
                        pltpu.VMEM((C_HEADS, C_HEAD_DIM, C_HEAD_DIM), F32),
                        pltpu.VMEM((C_HEADS, 1, C_HEAD_DIM), F32), pltpu.VMEM((C_HEADS, 1, 1), F32),
                        blk(), blk(), gate_blk(), gate_blk(), blk()],
        compiler_params=pltpu.CompilerParams(dimension_semantics=("parallel", "arbitrary"),
                                             vmem_limit_bytes=VMEM_LIMIT),
        name="mlstm_bwd_final" if final else "mlstm_fwd",
    )(*args)


def mlstm_mix(pc, p, tb):
    h_f = mlstm_pass(pc, p, None, reverse=False, tb=tb)
    return mlstm_pass(pc, p, h_f, reverse=True, tb=tb)


def _pad_cols(w, width):
    return jnp.pad(w, ((0, 0), (0, width - w.shape[1])))


def encoder_trunk(x, p, tb=256):
    bsz, t_len, d = x.shape
    n = bsz * t_len
    x2 = x.reshape(n, d)

    w_in = jnp.concatenate([p['even_w_in'][:, :A_IN], _pad_cols(p['even_w_in'][:, A_IN:], B_IN_PAD)], axis=1)
    pa, pb = norm_matmul(x2, p['norm_mix_g'][0], w_in.astype(BF16), (A_IN, B_IN_PAD))
    ya = rwkv_mix(pa.reshape(bsz, t_len, A_IN), p['rwkv'], tb)
    yb = gdn_mix(pb.reshape(bsz, t_len, B_IN_PAD), p['gdn'], tb)
    w_out = p['even_w_out'].astype(BF16)
    x2 = matmul_residual([ya.reshape(n, A_W), yb.reshape(n, B_W)], [w_out[:A_W], w_out[A_W:]], x2)
    x2 = ffn_residual(x2, p['norm_ffn_g'][0], p['ffn_w1'].astype(BF16), p['ffn_w3'].astype(BF16),
                      p['ffn_w2'].astype(BF16))

    (pc,) = norm_matmul(x2, p['norm_mix_g'][1], _pad_cols(p['odd_w_in'], C_IN_PAD).astype(BF16), (C_IN_PAD,))
    yc = mlstm_mix(pc.reshape(bsz, t_len, C_IN_PAD), p['mlstm'], tb)
    x2 = matmul_residual([yc.reshape(n, C_W)], [p['odd_w_out'].astype(BF16)], x2)
    x2 = moe_residual_final(x2, p['norm_ffn_g'][1], p['moe_router_w'], p['moe_router_b'],
                            p['moe_w1'].astype(BF16), p['moe_w3'].astype(BF16), p['moe_w2'].astype(BF16),
                            p['norm_final_g'])
    return x2.reshape(bsz, t_len, d)


def kernel(x_prompt, x_sample, norm_mix_g, norm_ffn_g, norm_final_g, even_w_in, rwkv_shift_mu, rwkv_w0, rwkv_w_up, rwkv_a0, rwkv_a_up, rwkv_g_up, rwkv_k_k, rwkv_k_a, rwkv_r_k, rwkv_ln_g, rwkv_ln_b, gdn_conv_w, gdn_A_log, gdn_dt_bias, gdn_norm_g, even_w_out, odd_w_in, mlstm_conv_w, mlstm_gate_b, mlstm_norm_g, odd_w_out, ffn_w1, ffn_w3, ffn_w2, moe_router_w, moe_router_b, moe_w1, moe_w3, moe_w2):
    p = dict(
        norm_mix_g=norm_mix_g, norm_ffn_g=norm_ffn_g, norm_final_g=norm_final_g,
        even_w_in=even_w_in[0], even_w_out=even_w_out[0], odd_w_in=odd_w_in[0], odd_w_out=odd_w_out[0],
        ffn_w1=ffn_w1[0], ffn_w3=ffn_w3[0], ffn_w2=ffn_w2[0],
        moe_router_w=moe_router_w[0], moe_router_b=moe_router_b[0],
        moe_w1=moe_w1[0], moe_w3=moe_w3[0], moe_w2=moe_w2[0],
        rwkv=dict(shift_mu=rwkv_shift_mu[0], w0=rwkv_w0[0], w_up=rwkv_w_up[0], a0=rwkv_a0[0], a_up=rwkv_a_up[0],
                  g_up=rwkv_g_up[0], k_k=rwkv_k_k[0], k_a=rwkv_k_a[0], r_k=rwkv_r_k[0], ln_g=rwkv_ln_g[0],
                  ln_b=rwkv_ln_b[0]),
        gdn=dict(conv_w=gdn_conv_w[0], A_log=gdn_A_log[0], dt_bias=gdn_dt_bias[0], norm_g=gdn_norm_g[0]),
        mlstm=dict(conv_w=mlstm_conv_w[0], gate_b=mlstm_gate_b[0], norm_g=mlstm_norm_g[0]),
    )
    return (encoder_trunk(x_prompt, p), encoder_trunk(x_sample, p))
```

```python
import functools

import jax
import jax.numpy as jnp
from jax import lax
from jax.experimental import pallas as pl
from jax.experimental.pallas import tpu as pltpu

F32 = jnp.float32
BF16 = jnp.bfloat16

D_MODEL = 1024
HEAD_DIM = 64
A_HEADS = 8
A_W = 512
W_RANK = 64
ICL_RANK = 64
G_RANK = 128
DECAY_SCALE = 0.606531
GN_EPS = 64e-5
A_IN = 1792
B_HEADS = 8
B_W = 512
C_HEADS = 8
C_HEAD_DIM = 128
C_W = 1024
CONV_W = 5
CHUNK = 64
D_FF = 2816
N_EXPERTS = 8
D_FF_EXPERT = 1408
HALO = 8
LANES = 128
PAIR = 2 * HEAD_DIM
N_PAIRS = A_W // PAIR
CHUNK_GROUP = 2
B_IN_PAD = 4 * B_W + LANES
C_IN_PAD = 4 * C_W + 2 * LANES

VMEM_LIMIT = 56 * 1024 * 1024

NT_DIMS = (((1,), (1,)), ((), ()))
TN_DIMS = (((0,), (0,)), ((), ()))


def _dot(a, b):
    return jnp.dot(a.astype(BF16), b.astype(BF16), preferred_element_type=F32)


def _dot_nt(a, b):
    return lax.dot_general(a.astype(BF16), b.astype(BF16), NT_DIMS, preferred_element_type=F32)


def _dot_tn(a, b):
    return lax.dot_general(a.astype(BF16), b.astype(BF16), TN_DIMS, preferred_element_type=F32)


def _split3(x):
    h = x.astype(BF16)
    r = x - h.astype(F32)
    m = r.astype(BF16)
    l = (r - m.astype(F32)).astype(BF16)
    return h, m, l


def _dot_exact_rhs(x, c):
    h, m, l = _split3(x)
    return (jnp.dot(h, c, preferred_element_type=F32) + jnp.dot(m, c, preferred_element_type=F32)
            + jnp.dot(l, c, preferred_element_type=F32))


def _dot_exact_lhs(c, x):
    h, m, l = _split3(x)
    return (jnp.dot(c, h, preferred_element_type=F32) + jnp.dot(c, m, preferred_element_type=F32)
            + jnp.dot(c, l, preferred_element_type=F32))


def _dot_tn_exact_rhs(x, c):
    h, m, l = _split3(x)
    f = lambda p: lax.dot_general(p, c, TN_DIMS, preferred_element_type=F32)
    return f(h) + f(m) + f(l)


def _dot3(a, b):
    ah = a.astype(BF16)
    al = (a - ah.astype(F32)).astype(BF16)
    bh = b.astype(BF16)
    bl = (b - bh.astype(F32)).astype(BF16)
    f = lambda p, q: jnp.dot(p, q, preferred_element_type=F32)
    return f(ah, bh) + f(ah, bl) + f(al, bh)


def _sigmoid(x):
    return 1.0 / (1.0 + jnp.exp(-x))


def _silu(x):
    return x * _sigmoid(x)


def _softplus(x):
    return jnp.maximum(x, 0.0) + jnp.log(1.0 + jnp.exp(-jnp.abs(x)))


def _norm_matmul_kernel(x_ref, g_ref, w_ref, *out_refs, widths, tn):
    x = x_ref[...]
    h = (x * lax.rsqrt(jnp.mean(x * x, axis=-1, keepdims=True) + 1e-6) * g_ref[...]).astype(BF16)
    off = 0
    for o_ref, wd in zip(out_refs, widths):
        for c in range(0, wd, tn):
            cw = min(tn, wd - c)
            o_ref[:, c:c + cw] = jnp.dot(h, w_ref[:, off + c:off + c + cw], preferred_element_type=F32)
        off += wd


def norm_matmul(x2d, g, w, widths, tm=256, tn=512):
    n, d = x2d.shape
    ntot = w.shape[1]
    assert sum(widths) == ntot and n % tm == 0
    return pl.pallas_call(
        functools.partial(_norm_matmul_kernel, widths=widths, tn=tn),
        out_shape=[jax.ShapeDtypeStruct((n, wd), F32) for wd in widths],
        grid=(n // tm,),
        in_specs=[pl.BlockSpec((tm, d), lambda i: (i, 0)),
                  pl.BlockSpec((1, d), lambda i: (0, 0)),
                  pl.BlockSpec((d, ntot), lambda i: (0, 0))],
        out_specs=[pl.BlockSpec((tm, wd), lambda i: (i, 0)) for wd in widths],
        compiler_params=pltpu.CompilerParams(dimension_semantics=("parallel",), vmem_limit_bytes=VMEM_LIMIT),
        name="norm_matmul",
    )(x2d, g.reshape(1, d), w)


def _matmul_res_kernel(*refs, n_in):
    y_refs = refs[:n_in]
    w_refs = refs[n_in:2 * n_in]
    x_ref, o_ref = refs[2 * n_in], refs[2 * n_in + 1]
    acc = x_ref[...]
    for y_ref, w_ref in zip(y_refs, w_refs):
        acc = acc + jnp.dot(y_ref[...], w_ref[...], preferred_element_type=F32)
    o_ref[...] = acc


def matmul_residual(ys, ws, x2d, tm=512):
    n, d = x2d.shape
    assert n % tm == 0
    return pl.pallas_call(
        functools.partial(_matmul_res_kernel, n_in=len(ys)),
        out_shape=jax.ShapeDtypeStruct((n, d), F32),
        grid=(n // tm,),
        in_specs=([pl.BlockSpec((tm, y.shape[1]), lambda i: (i, 0)) for y in ys]
                  + [pl.BlockSpec(w.shape, lambda i: (0, 0)) for w in ws]
                  + [pl.BlockSpec((tm, d), lambda i: (i, 0))]),
        out_specs=pl.BlockSpec((tm, d), lambda i: (i, 0)),
        compiler_params=pltpu.CompilerParams(dimension_semantics=("parallel",), vmem_limit_bytes=VMEM_LIMIT),
        name="matmul_residual",
    )(*ys, *ws, x2d)


def _ff_chunks(width, tf):
    return [(c, min(tf, width - c)) for c in range(0, width, tf)]


def _ffn_kernel(x_ref, g_ref, w1_ref, w3_ref, w2_ref, o_ref, *, tf):
    x = x_ref[...]
    h = (x * lax.rsqrt(jnp.mean(x * x, axis=-1, keepdims=True) + 1e-6) * g_ref[...]).astype(BF16)
    acc = x
    for c, cw in _ff_chunks(D_FF, tf):
        a = jnp.dot(h, w1_ref[:, c:c + cw], preferred_element_type=F32)
        b = jnp.dot(h, w3_ref[:, c:c + cw], preferred_element_type=F32)
        acc = acc + jnp.dot((_silu(a) * b).astype(BF16), w2_ref[c:c + cw, :], preferred_element_type=F32)
    o_ref[...] = acc


def ffn_residual(x2d, g, w1, w3, w2, tm=512, tf=256):
    n, d = x2d.shape
    assert n % tm == 0
    const = lambda shape: pl.BlockSpec(shape, lambda i: (0, 0), pipeline_mode=pl.Buffered(1))
    return pl.pallas_call(
        functools.partial(_ffn_kernel, tf=tf),
        out_shape=jax.ShapeDtypeStruct((n, d), F32),
        grid=(n // tm,),
        in_specs=[pl.BlockSpec((tm, d), lambda i: (i, 0)), const((1, d)),
                  const(w1.shape), const(w3.shape), const(w2.shape)],
        out_specs=pl.BlockSpec((tm, d), lambda i: (i, 0)),
        compiler_params=pltpu.CompilerParams(dimension_semantics=("parallel",), vmem_limit_bytes=VMEM_LIMIT),
        name="ffn_residual",
    )(x2d, g.reshape(1, d), w1, w3, w2)


def _moe_kernel(x_ref, g_ref, rw_ref, rb_ref, w1_ref, w3_ref, w2_ref, gf_ref, o_ref, h_ref, gate_ref, acc_ref, *, tf):
    e = pl.program_id(1)

    @pl.when(e == 0)
    def _():
        x = x_ref[...]
        hf = x * lax.rsqrt(jnp.mean(x * x, axis=-1, keepdims=True) + 1e-6) * g_ref[...]
        h_ref[...] = hf.astype(BF16)
        logits = _dot3(hf, rw_ref[...]) + rb_ref[...]
        lane = lax.broadcasted_iota(jnp.int32, logits.shape, 1)
        m1 = jnp.max(logits, axis=-1, keepdims=True)
        i1 = jnp.min(jnp.where(logits == m1, lane, LANES), axis=-1, keepdims=True)
        rest = jnp.where(lane == i1, -jnp.inf, logits)
        m2 = jnp.max(rest, axis=-1, keepdims=True)
        i2 = jnp.min(jnp.where(rest == m2, lane, LANES), axis=-1, keepdims=True)
        e2 = jnp.exp(m2 - m1)
        den = 1.0 + e2
        gate_ref[...] = jnp.where(lane == i1, 1.0 / den, jnp.where(lane == i2, e2 / den, 0.0))
        acc_ref[...] = x

    lane = lax.broadcasted_iota(jnp.int32, gate_ref.shape, 1)
    ge = jnp.sum(jnp.where(lane == e, gate_ref[...], 0.0), axis=-1, keepdims=True)
    h = h_ref[...]
    y = jnp.zeros(acc_ref.shape, F32)
    for c, cw in _ff_chunks(D_FF_EXPERT, tf):
        a = jnp.dot(h, w1_ref[:, c:c + cw], preferred_element_type=F32)
        b = jnp.dot(h, w3_ref[:, c:c + cw], preferred_element_type=F32)
        y = y + jnp.dot((_silu(a) * b).astype(BF16), w2_ref[c:c + cw, :], preferred_element_type=F32)
    acc_ref[...] += ge * y

    @pl.when(e == N_EXPERTS - 1)
    def _():
        z = acc_ref[...]
        o_ref[...] = z * lax.rsqrt(jnp.mean(z * z, axis=-1, keepdims=True) + 1e-6) * gf_ref[...]


def moe_residual_final(x2d, g, router_w, router_b, w1, w3, w2, g_final, tm=1024, tf=256):
    n, d = x2d.shape
    assert n % tm == 0
    rw = jnp.zeros((d, LANES), F32).at[:, :N_EXPERTS].set(router_w)
    rb = jnp.full((1, LANES), -1e30, F32).at[0, :N_EXPERTS].set(router_b)
    return pl.pallas_call(
        functools.partial(_moe_kernel, tf=tf),
        out_shape=jax.ShapeDtypeStruct((n, d), F32),
        grid=(n // tm, N_EXPERTS),
        in_specs=[pl.BlockSpec((tm, d), lambda i, e: (i, 0)),
                  pl.BlockSpec((1, d), lambda i, e: (0, 0)),
                  pl.BlockSpec((d, LANES), lambda i, e: (0, 0)),
                  pl.BlockSpec((1, LANES), lambda i, e: (0, 0)),
                  pl.BlockSpec((None, d, D_FF_EXPERT), lambda i, e: (e, 0, 0)),
                  pl.BlockSpec((None, d, D_FF_EXPERT), lambda i, e: (e, 0, 0)),
                  pl.BlockSpec((None, D_FF_EXPERT, d), lambda i, e: (e, 0, 0)),
                  pl.BlockSpec((1, d), lambda i, e: (0, 0))],
        out_specs=pl.BlockSpec((tm, d), lambda i, e: (i, 0)),
        scratch_shapes=[pltpu.VMEM((tm, d), BF16), pltpu.VMEM((tm, LANES), F32), pltpu.VMEM((tm, d), F32)],
        compiler_params=pltpu.CompilerParams(dimension_semantics=("parallel", "arbitrary"),
                                             vmem_limit_bytes=VMEM_LIMIT),
        name="moe_residual_final",
    )(x2d, g.reshape(1, d), rw, rb, w1, w3, w2, g_final.reshape(1, d))


def _seq_specs(tb, nt, width, reverse, col_block=0):
    hb = tb // HALO
    nhb = nt * hb
    pos = (lambda t: nt - 1 - t) if reverse else (lambda t: t)
    main = pl.BlockSpec((None, tb, width), lambda b, t: (b, pos(t), col_block))
    prev = pl.BlockSpec((None, HALO, width), lambda b, t: (b, jnp.maximum(pos(t) * hb - 1, 0), col_block))
    nxt = pl.BlockSpec((None, HALO, width), lambda b, t: (b, jnp.minimum((pos(t) + 1) * hb, nhb - 1), col_block))
    return main, prev, nxt


def _fill_extended(xe_ref, x_ref, xp_ref, xn_ref, pos, nt, tb, width=None):
    sl = slice(None) if width is None else slice(0, width)
    xe_ref[0:HALO, :] = jnp.where(pos == 0, 0.0, xp_ref[:, sl])
    xe_ref[HALO:HALO + tb, :] = x_ref[:, sl]
    xe_ref[HALO + tb:2 * HALO + tb, :] = jnp.where(pos == nt - 1, 0.0, xn_ref[:, sl])


def _seg_mats():
    r = lax.broadcasted_iota(jnp.int32, (A_W, A_W), 0) // HEAD_DIM
    c = lax.broadcasted_iota(jnp.int32, (A_W, A_W), 1) // HEAD_DIM
    return (r == c).astype(BF16)


def _tri_const(reverse, reps=1):
    r = jnp.arange(CHUNK)[:, None]
    c = jnp.arange(CHUNK)[None, :]
    m = ((c >= r) if reverse else (c <= r)).astype(BF16)
    return jnp.tile(m, (1, reps))


def _pair_masks(reverse):
    row = lax.broadcasted_iota(jnp.int32, (CHUNK, PAIR), 0)
    lane = lax.broadcasted_iota(jnp.int32, (CHUNK, PAIR), 1)
    col = lane % HEAD_DIM
    strict = (col > row) if reverse else (col < row)
    incl = (col >= row) if reverse else (col <= row)
    eye = (col == row).astype(F32)
    r2 = lax.broadcasted_iota(jnp.int32, (PAIR, PAIR), 0) // HEAD_DIM
    c2 = lax.broadcasted_iota(jnp.int32, (PAIR, PAIR), 1) // HEAD_DIM
    return strict, incl, eye, r2 == c2, lane < HEAD_DIM


def _bd(x2, bdm):
    return jnp.where(bdm, jnp.concatenate([x2, x2], axis=0), 0.0).astype(BF16)


def _mm(a, b):
    return jnp.dot(a.astype(BF16), b, preferred_element_type=F32)


def _tri_inverse_levels(n2s, eye2, bdm, out):
    ps = [_mm(n, _bd(n, bdm)) for n in n2s]
    ts = [eye2 + n for n in n2s]
    yield
    for i in range(5):
        if i < 4:
            outs = [_mm(p, jnp.concatenate([_bd(t, bdm), _bd(p, bdm)], axis=1)) for t, p in zip(ts, ps)]
            ps = [o[:, PAIR:] for o in outs]
        else:
            outs = [_mm(p, _bd(t, bdm)) for t, p in zip(ts, ps)]
        ts = [t + o[:, 0:PAIR] for t, o in zip(ts, outs)]
        yield
    out.extend(ts)


def _run_interleaved(*gens):
    live = [g for g in gens if g is not None]
    while live:
        for g in list(live):
            try:
                next(g)
            except StopIteration:
                live.remove(g)


def _software_pipeline(n_groups, phase_a, chain):
    _run_interleaved(phase_a(0))
    for g in range(n_groups):
        _run_interleaved(chain(g), phase_a(g + 1) if g + 1 < n_groups else None)


def _rwkv_kernel(*refs, reverse, final, tb, nt, group):
    if final:
        (x_ref, xp_ref, xn_ref, mu_ref, kk_ref, ka_ref, w0_ref, wup_ref, a0_ref, aup_ref, seg_ref, tri_ref,
         gup_ref, rk_ref, lng_ref, lnb_ref, yf_ref, o_ref,
         xe_ref, s_ref, r_s, k_s, v_s, a_s, b_s, lw_s, y_s) = refs
    else:
        (x_ref, xp_ref, xn_ref, mu_ref, kk_ref, ka_ref, w0_ref, wup_ref, a0_ref, aup_ref, seg_ref, tri_ref,
         o_ref, xe_ref, s_ref, r_s, k_s, v_s, a_s, b_s, lw_s, y_s) = refs
    d = 1 if reverse else 0
    t = pl.program_id(1)
    pos = (nt - 1 - t) if reverse else t
    nc = tb // CHUNK

    @pl.when(t == 0)
    def _():
        s_ref[...] = jnp.zeros(s_ref.shape, F32)

    _fill_extended(xe_ref, x_ref, xp_ref, xn_ref, pos, nt, tb)
    x = x_ref[...]
    nb = 0.5 * (xe_ref[pl.ds(HALO - 1, tb), :] + xe_ref[pl.ds(HALO + 1, tb), :])
    pa = x + mu_ref[...] * (nb - x)
    r = pa[:, 0:A_W]
    k = pa[:, A_W:2 * A_W]
    v = pa[:, 2 * A_W:3 * A_W]
    wd = jnp.tanh(pa[:, 3 * A_W:3 * A_W + W_RANK])
    ad = pa[:, 3 * A_W + W_RANK:3 * A_W + W_RANK + ICL_RANK]
    seg = seg_ref[...]
    kq = k * kk_ref[...]
    kk = kq * lax.rsqrt(_dot_exact_rhs(kq * kq, seg) + 1e-6)
    lw = -DECAY_SCALE * _sigmoid(w0_ref[d:d + 1, :] + _dot(wd, wup_ref[d]))
    icl = _sigmoid(a0_ref[d:d + 1, :] + _dot(ad, aup_ref[d]))
    ka = ka_ref[...]
    r_s[...] = r
    k_s[...] = k * (1.0 + (icl - 1.0) * ka)
    v_s[...] = v
    a_s[...] = -kk
    b_s[...] = kk * icl
    lw_s[...] = lw

    strict, incl, eye2, bdm, _ = _pair_masks(reverse)
    tri = tri_ref[...]

    groups = [list(range(nc))[j:j + group] for j in range(0, nc, group)]
    if reverse:
        groups = [[nc - 1 - c for c in g] for g in groups]
    staged = {}

    def phase_a(g):
        probs = []
        for c in groups[g]:
            rows = slice(c * CHUNK, (c + 1) * CHUNK)
            lwc = lw_s[rows, :]
            cum = _dot_exact_lhs(tri, lwc)
            last = cum[0:1, :] if reverse else cum[CHUNK - 1:CHUNK, :]
            rt = r_s[rows, :] * jnp.exp(cum)
            at = a_s[rows, :] * jnp.exp(cum - lwc)
            einv = jnp.exp(-cum)
            bc = b_s[rows, :]
            kc = k_s[rows, :]
            bt = bc * einv
            kt = kc * einv
            eend = jnp.exp(last - cum)
            bh = bc * eend
            kh = kc * eend
            gl = jnp.exp(last)
            vc = v_s[rows, :]
            for p in range(N_PAIRS):
                sl = slice(p * PAIR, (p + 1) * PAIR)
                probs.append(dict(c=c, p=p, at=at[:, sl], rt=rt[:, sl], bt=bt[:, sl], kt=kt[:, sl],
                                  bk=jnp.concatenate([bh[:, sl], kh[:, sl]], axis=0).astype(BF16),
                                  gl=gl[:, sl], v=vc[:, sl]))
        yield
        for q in probs:
            xm = jnp.concatenate([q['at'], q['rt']], axis=0).astype(BF16)
            wm = jnp.concatenate([_bd(q.pop('bt'), bdm), _bd(q.pop('kt'), bdm)], axis=0)
            pm = lax.dot_general(xm, wm, NT_DIMS, preferred_element_type=F32)
            q['a_ab'] = jnp.where(strict, pm[0:CHUNK, 0:PAIR], 0.0)
            q['a_kr'] = jnp.concatenate([jnp.where(strict, pm[0:CHUNK, PAIR:], 0.0),
                                         jnp.where(incl, pm[CHUNK:, PAIR:], 0.0)], axis=0).astype(BF16)
            q['a_rb'] = jnp.where(incl, pm[CHUNK:, 0:PAIR], 0.0).astype(BF16)
        yield
        tinvs = []
        yield from _tri_inverse_levels([q.pop('a_ab') for q in probs], eye2, bdm, tinvs)
        for q in probs:
            q['wy'] = _mm(q.pop('a_kr'), _bd(q['v'], bdm))
        yield
        for q, tinv in zip(probs, tinvs):
            wy = q.pop('wy')
            ua = _mm(tinv, jnp.concatenate([_bd(wy[0:CHUNK], bdm), _bd(q.pop('at'), bdm)], axis=1))
            q['u0'] = ua[:, 0:PAIR]
            q['ar'] = jnp.concatenate([ua[:, PAIR:], q.pop('rt')], axis=0).astype(BF16)
            q['y0'] = wy[CHUNK:]
        yield
        staged[g] = probs

    def chain(g):
        probs = staged.pop(g)
        for j in range(0, len(probs), N_PAIRS):
            cp = probs[j:j + N_PAIRS]
            ss = [s_ref[q['p']] for q in cp]
            uys = [lax.dot_general(q['ar'], s.astype(BF16), NT_DIMS, preferred_element_type=F32)
                   for q, s in zip(cp, ss)]
            yield
            ys = []
            for q, s, uy in zip(cp, ss, uys):
                u = uy[0:CHUNK] + q['u0']
                ys.append(uy[CHUNK:] + _mm(q['a_rb'], _bd(u, bdm)) + q['y0'])
                upd = lax.dot_general(jnp.concatenate([u, q['v']], axis=0).astype(BF16), q['bk'], TN_DIMS,
                                      preferred_element_type=F32)
                s_ref[q['p']] = s * q['gl'] + jnp.where(bdm, upd, 0.0)
            c = cp[0]['c']
            y_s[c * CHUNK:(c + 1) * CHUNK, :] = jnp.concatenate(ys, axis=1)
            yield

    _software_pipeline(len(groups), phase_a, chain)

    if not final:
        o_ref[...] = y_s[...]
    else:
        y = yf_ref[...] + y_s[...]
        seg_mean = lambda z: _dot_exact_rhs(z, seg) * (1.0 / HEAD_DIM)
        mu = seg_mean(y)
        yc = y - mu
        var = seg_mean(yc * yc)
        yn = yc * lax.rsqrt(var + GN_EPS) * lng_ref[...] + lnb_ref[...]
        o = 1 - d
        icl_o = _sigmoid(a0_ref[o:o + 1, :] + _dot(ad, aup_ref[o]))
        ksum = k * (2.0 + (icl + icl_o - 2.0) * ka)
        bonus = _dot_exact_rhs(r * ksum * rk_ref[...], seg) * v
        gate = _dot(_sigmoid(pa[:, 3 * A_W + W_RANK + ICL_RANK:]), gup_ref[...])
        o_ref[...] = ((yn + bonus) * gate).astype(o_ref.dtype)


def rwkv_pass(pa, p, y_fwd, *, reverse, tb):
    bsz, t_len, _ = pa.shape
    nt = t_len // tb
    final = y_fwd is not None
    main, prev, nxt = _seq_specs(tb, nt, A_IN, reverse)
    const2 = lambda a: pl.BlockSpec(a.shape, lambda b, t: (0,) * a.ndim)
    row = lambda a: a.reshape(1, -1).astype(F32)
    consts = [row(p['shift_mu']), row(p['k_k']), row(p['k_a']), p['w0'].astype(F32), p['w_up'].astype(BF16),
              p['a0'].astype(F32), p['a_up'].astype(BF16), _seg_mats(), _tri_const(reverse)]
    args = [pa, pa, pa] + consts
    in_specs = [main, prev, nxt] + [const2(a) for a in consts]
    pos = (lambda t: nt - 1 - t) if reverse else (lambda t: t)
    io_spec = pl.BlockSpec((None, tb, A_W), lambda b, t: (b, pos(t), 0))
    if final:
        extra = [p['g_up'].astype(BF16), row(p['r_k']), row(p['ln_g']), row(p['ln_b'])]
        args += extra + [y_fwd]
        in_specs += [const2(a) for a in extra] + [io_spec]
    blk = lambda: pltpu.VMEM((tb, A_W), F32)
    return pl.pallas_call(
        functools.partial(_rwkv_kernel, reverse=reverse, final=final, tb=tb, nt=nt, group=CHUNK_GROUP),
        out_shape=jax.ShapeDtypeStruct((bsz, t_len, A_W), BF16 if final else F32),
        grid=(bsz, nt),
        in_specs=in_specs,
        out_specs=io_spec,
        scratch_shapes=[pltpu.VMEM((tb + 2 * HALO, A_IN), F32), pltpu.VMEM((N_PAIRS, PAIR, PAIR), F32),
                        blk(), blk(), blk(), blk(), blk(), blk(), blk()],
        compiler_params=pltpu.CompilerParams(dimension_semantics=("parallel", "arbitrary"),
                                             vmem_limit_bytes=VMEM_LIMIT),
        name="rwkv_bwd_final" if final else "rwkv_fwd",
    )(*args)


def rwkv_mix(pa, p, tb):
    y_f = rwkv_pass(pa, p, None, reverse=False, tb=tb)
    return rwkv_pass(pa, p, y_f, reverse=True, tb=tb)


def _gdn_kernel(*refs, reverse, final, tb, nt, group):
    if final:
        (x_ref, xp_ref, xn_ref, cw_ref, alog_ref, dtb_ref, seg_ref, tri_ref, trit_ref, exg_ref, exb_ref,
         ng_ref, of_ref, o_ref, xe_ref, s_ref, q_s, k_s, v_s, g_s, bx_s, y_s) = refs
    else:
        (x_ref, xp_ref, xn_ref, cw_ref, alog_ref, dtb_ref, seg_ref, tri_ref, trit_ref, exg_ref, exb_ref,
         o_ref, xe_ref, s_ref, q_s, k_s, v_s, g_s, bx_s, y_s) = refs
    d = 1 if reverse else 0
    t = pl.program_id(1)
    pos = (nt - 1 - t) if reverse else t
    nc = tb // CHUNK
    qkv_w = 3 * B_W

    @pl.when(t == 0)
    def _():
        s_ref[...] = jnp.zeros(s_ref.shape, F32)

    _fill_extended(xe_ref, x_ref, xp_ref, xn_ref, pos, nt, tb, width=qkv_w)
    conv = xe_ref[pl.ds(HALO - 2, tb), :] * cw_ref[0:1, :]
    for j in range(1, CONV_W):
        conv = conv + xe_ref[pl.ds(HALO - 2 + j, tb), :] * cw_ref[j:j + 1, :]
    qkv = _silu(conv)
    seg = seg_ref[...]
    q = qkv[:, 0:B_W]
    k = qkv[:, B_W:2 * B_W]
    q_s[...] = q * lax.rsqrt(_dot_exact_rhs(q * q, seg) + 1e-6) * (HEAD_DIM ** -0.5)
    k_s[...] = k * lax.rsqrt(_dot_exact_rhs(k * k, seg) + 1e-6)
    v_s[...] = qkv[:, 2 * B_W:]
    gl = x_ref[:, 4 * B_W:4 * B_W + LANES]
    g_s[...] = -jnp.exp(alog_ref[...]) * _softplus(gl + dtb_ref[...])
    bx_s[...] = _dot_exact_rhs(_sigmoid(gl), exb_ref[...])

    strict, incl, eye2, bdm, left = _pair_masks(reverse)
    tri = tri_ref[...]
    trit2 = trit_ref[...]
    exg = exg_ref[...]

    groups = [list(range(nc))[j:j + group] for j in range(0, nc, group)]
    if reverse:
        groups = [[nc - 1 - c for c in g] for g in groups]
    staged = {}

    def phase_a(g):
        probs = []
        for c in groups[g]:
            rows = slice(c * CHUNK, (c + 1) * CHUNK)
            gch = g_s[rows, :]
            gc = _dot_exact_lhs(tri, gch)
            gct = _dot_tn_exact_rhs(gch, trit2)
            last = gc[0:1, :] if reverse else gc[CHUNK - 1:CHUNK, :]
            eg = _dot_exact_rhs(jnp.exp(gc), exg)
            eend = _dot_exact_rhs(jnp.exp(last - gc), exg)
            glast = _dot_exact_rhs(jnp.exp(last), exg)
            qc = q_s[rows, :]
            kc = k_s[rows, :]
            vc = v_s[rows, :]
            beta = bx_s[rows, :]
            kb = kc * beta
            vb = vc * beta
            kbe = kb * eg
            qd = qc * eg
            kd = kc * eend
            for p in range(N_PAIRS):
                sl = slice(p * PAIR, (p + 1) * PAIR)
                l0 = d * B_HEADS + 2 * p
                diff = (jnp.where(left, gc[:, l0:l0 + 1], gc[:, l0 + 1:l0 + 2])
                        - jnp.where(left[0:1], gct[l0:l0 + 1, :], gct[l0 + 1:l0 + 2, :]))
                probs.append(dict(c=c, p=p, decay=jnp.exp(jnp.where(incl, diff, 0.0)),
                                  kq=jnp.concatenate([kb[:, sl], qc[:, sl]], axis=0).astype(BF16), k=kc[:, sl],
                                  vb=vb[:, sl], kbe=kbe[:, sl], qd=qd[:, sl], kd=kd[:, sl].astype(BF16),
                                  gl=glast[:, sl]))
        yield
        for q in probs:
            pm = lax.dot_general(q.pop('kq'), _bd(q.pop('k'), bdm), NT_DIMS, preferred_element_type=F32)
            decay = q.pop('decay')
            q['n'] = jnp.where(strict, -pm[0:CHUNK] * decay, 0.0)
            q['qk'] = jnp.where(incl, pm[CHUNK:] * decay, 0.0).astype(BF16)
        yield
        tinvs = []
        yield from _tri_inverse_levels([q.pop('n') for q in probs], eye2, bdm, tinvs)
        for q, tinv in zip(probs, tinvs):
            sol = _mm(tinv, jnp.concatenate([_bd(q.pop('vb'), bdm), _bd(q.pop('kbe'), bdm)], axis=1))
            q['u'] = sol[:, 0:PAIR]
            q['wq'] = jnp.concatenate([sol[:, PAIR:], q.pop('qd')], axis=0).astype(BF16)
        yield
        staged[g] = probs

    def chain(g):
        probs = staged.pop(g)
        for j in range(0, len(probs), N_PAIRS):
            cp = probs[j:j + N_PAIRS]
            ss = [s_ref[q['p']] for q in cp]
            wss = [jnp.dot(q['wq'], s.astype(BF16), preferred_element_type=F32) for q, s in zip(cp, ss)]
            yield
            ys = []
            for q, s, ws in zip(cp, ss, wss):
                v_new = q['u'] - ws[0:CHUNK]
                ys.append(ws[CHUNK:] + _mm(q['qk'], _bd(v_new, bdm)))
                upd = lax.dot_general(q['kd'], v_new.astype(BF16), TN_DIMS, preferred_element_type=F32)
                s_ref[q['p']] = s * q['gl'] + jnp.where(bdm, upd, 0.0)
            c = cp[0]['c']
            y_s[c * CHUNK:(c + 1) * CHUNK, :] = jnp.concatenate(ys, axis=1)
            yield

    _software_pipeline(len(groups), phase_a, chain)

    if not final:
        o_ref[...] = y_s[...]
    else:
        o = of_ref[...] + y_s[...]
        ms = _dot_exact_rhs(o * o, seg) * (1.0 / HEAD_DIM)
        z = x_ref[:, 3 * B_W:4 * B_W]
        o_ref[...] = (o * lax.rsqrt(ms + 1e-6) * ng_ref[...] * _silu(z)).astype(o_ref.dtype)


def _expand_mat(first_lane):
    r = jnp.arange(LANES)[:, None]
    c = jnp.arange(B_W)[None, :] // HEAD_DIM
    return (r == c + first_lane).astype(BF16)


def gdn_pass(pb, p, o_fwd, *, reverse, tb):
    bsz, t_len, _ = pb.shape
    nt = t_len // tb
    final = o_fwd is not None
    d = 1 if reverse else 0
    main, prev, nxt = _seq_specs(tb, nt, B_IN_PAD, reverse)
    const2 = lambda a: pl.BlockSpec(a.shape, lambda b, t: (0,) * a.ndim)
    pad_row = lambda a: jnp.zeros((1, LANES), F32).at[0, :a.size].set(a.reshape(-1))
    consts = [p['conv_w'].astype(F32), pad_row(p['A_log']), pad_row(p['dt_bias']), _seg_mats(),
              _tri_const(reverse), _tri_const(not reverse, reps=2), _expand_mat(d * B_HEADS),
              _expand_mat(2 * B_HEADS + d * B_HEADS)]
    args = [pb, pb, pb] + consts
    in_specs = [main, prev, nxt] + [const2(a) for a in consts]
    pos = (lambda t: nt - 1 - t) if reverse else (lambda t: t)
    io_spec = pl.BlockSpec((None, tb, B_W), lambda b, t: (b, pos(t), 0))
    if final:
        ng = jnp.tile(p['norm_g'].astype(F32), B_HEADS).reshape(1, B_W)
        args += [ng, o_fwd]
        in_specs += [const2(ng), io_spec]
    blk = lambda: pltpu.VMEM((tb, B_W), F32)
    return pl.pallas_call(
        functools.partial(_gdn_kernel, reverse=reverse, final=final, tb=tb, nt=nt, group=CHUNK_GROUP),
        out_shape=jax.ShapeDtypeStruct((bsz, t_len, B_W), BF16 if final else F32),
        grid=(bsz, nt),
        in_specs=in_specs,
        out_specs=io_spec,
        scratch_shapes=[pltpu.VMEM((tb + 2 * HALO, 3 * B_W), F32), pltpu.VMEM((N_PAIRS, PAIR, PAIR), F32),
                        blk(), blk(), blk(), pltpu.VMEM((tb, LANES), F32), blk(), blk()],
        compiler_params=pltpu.CompilerParams(dimension_semantics=("parallel", "arbitrary"),
                                             vmem_limit_bytes=VMEM_LIMIT),
        name="gdn_bwd_final" if final else "gdn_fwd",
    )(*args)


def gdn_mix(pb, p, tb):
    o_f = gdn_pass(pb, p, None, reverse=False, tb=tb)
    return gdn_pass(pb, p, o_f, reverse=True, tb=tb)


def _mlstm_kernel(*refs, reverse, final, tb, nt):
    if final:
        (x_ref, xp_ref, xn_ref, cw_ref, bi_ref, bf_ref, tri_ref, trit_ref, ng_ref, hf_ref, o_ref,
         xe_ref, c_ref, n_ref, m_ref, q_s, k_s, li_s, lf_s, y_s) = refs
    else:
        (x_ref, xp_ref, xn_ref, cw_ref, bi_ref, bf_ref, tri_ref, trit_ref, o_ref,
         xe_ref, c_ref, n_ref, m_ref, q_s, k_s, li_s, lf_s, y_s) = refs
    d = 1 if reverse else 0
    t = pl.program_id(1)
    pos = (nt - 1 - t) if reverse else t
    nc = tb // CHUNK
    qk_w = 2 * C_W
    dh = C_HEAD_DIM

    @pl.when(t == 0)
    def _():
        c_ref[...] = jnp.zeros(c_ref.shape, F32)
        n_ref[...] = jnp.zeros(n_ref.shape, F32)
        m_ref[...] = jnp.zeros(m_ref.shape, F32)

    _fill_extended(xe_ref, x_ref, xp_ref, xn_ref, pos, nt, tb, width=qk_w)
    conv = xe_ref[pl.ds(HALO - 2, tb), :] * cw_ref[0:1, :]
    for j in range(1, CONV_W):
        conv = conv + xe_ref[pl.ds(HALO - 2 + j, tb), :] * cw_ref[j:j + 1, :]
    qk = _silu(conv)
    q_s[...] = qk[:, 0:C_W] * (dh ** -0.5)
    k_s[...] = qk[:, C_W:]
    li_s[...] = x_ref[:, 4 * C_W:4 * C_W + LANES] + bi_ref[...]
    lf_s[...] = -_softplus(-(x_ref[:, 4 * C_W + LANES:4 * C_W + 2 * LANES] + bf_ref[...]))

    row = lax.broadcasted_iota(jnp.int32, (CHUNK, CHUNK), 0)
    col = lax.broadcasted_iota(jnp.int32, (CHUNK, CHUNK), 1)
    incl = (col >= row) if reverse else (col <= row)
    eye_b = (col == row).astype(BF16)
    tri = tri_ref[...]
    trit = trit_ref[...]

    for i in range(nc):
        c = (nc - 1 - i) if reverse else i
        rows = slice(c * CHUNK, (c + 1) * CHUNK)
        lfc = lf_s[rows, :]
        lic = li_s[rows, :]
        bcum = _dot_exact_lhs(tri, lfc)
        bcum_t = _dot_tn_exact_rhs(lfc, trit)
        li_t = _dot_tn_exact_rhs(lic, eye_b)
        last = bcum[0:1, :] if reverse else bcum[CHUNK - 1:CHUNK, :]
        m = m_ref[...]
        w_end = last - bcum + lic
        m_new = jnp.maximum(last + m, jnp.max(w_end, axis=0, keepdims=True))
        s_old = jnp.exp(last + m - m_new)
        sk_scale = jnp.exp(w_end - m_new)
        a_inter = bcum + m
        m_ref[...] = m_new
        qc = q_s[rows, :]
        kc = k_s[rows, :]
        vc = x_ref[rows, 2 * C_W:3 * C_W]
        hd = []
        for h in range(C_HEADS):
            sl = slice(h * dh, (h + 1) * dh)
            l = d * C_HEADS + h
            qh, kh, vh = qc[:, sl].astype(BF16), kc[:, sl], vc[:, sl].astype(BF16)
            cm = c_ref[h]
            nm = n_ref[h]
            s_k = sk_scale[:, l:l + 1] * kh
            hd.append(dict(
                h=h, l=l, vh=vh, cm=cm, nm=nm,
                qk=lax.dot_general(qh, kh.astype(BF16), NT_DIMS, preferred_element_type=F32),
                qc=jnp.dot(qh, cm.astype(BF16), preferred_element_type=F32),
                kv=lax.dot_general(s_k.astype(BF16), vh, TN_DIMS, preferred_element_type=F32),
                qn=jnp.sum(qc[:, sl] * nm, axis=-1, keepdims=True),
                ksum=jnp.sum(s_k, axis=0, keepdims=True)))
        for e in hd:
            l = e['l']
            dmat = jnp.where(incl, bcum[:, l:l + 1] - bcum_t[l:l + 1, :] + li_t[l:l + 1, :], -jnp.inf)
            m_row = jnp.maximum(a_inter[:, l:l + 1], jnp.max(dmat, axis=-1, keepdims=True))
            e['s_inter'] = jnp.exp(a_inter[:, l:l + 1] - m_row)
            e['m_row'] = m_row
            e['pm'] = e.pop('qk') * jnp.exp(dmat - m_row)
            e['pv'] = _mm(e['pm'], e['vh'])
        ys = []
        for e in hd:
            h, l = e['h'], e['l']
            num = e['s_inter'] * e['qc'] + e['pv']
            den = e['s_inter'] * e['qn'] + jnp.sum(e['pm'], axis=-1, keepdims=True)
            ys.append(num / jnp.maximum(jnp.abs(den), jnp.exp(-e['m_row'])))
            c_ref[h] = s_old[:, l:l + 1] * e['cm'] + e['kv']
            n_ref[h] = s_old[:, l:l + 1] * e['nm'] + e['ksum']
        y_s[rows, :] = jnp.concatenate(ys, axis=1)

    if not final:
        o_ref[...] = y_s[...]
    else:
        hs = hf_ref[...] + y_s[...]
        outs = []
        for h in range(C_HEADS):
            sl = slice(h * dh, (h + 1) * dh)
            hh = hs[:, sl]
            outs.append(hh * lax.rsqrt(jnp.mean(hh * hh, axis=-1, keepdims=True) + 1e-6))
        hn = jnp.concatenate(outs, axis=1) * ng_ref[...]
        og = x_ref[:, 3 * C_W:4 * C_W]
        o_ref[...] = (_sigmoid(og) * hn).astype(o_ref.dtype)


def mlstm_pass(pc, p, h_fwd, *, reverse, tb):
    bsz, t_len, _ = pc.shape
    nt = t_len // tb
    final = h_fwd is not None
    main, prev, nxt = _seq_specs(tb, nt, C_IN_PAD, reverse)
    const2 = lambda a: pl.BlockSpec(a.shape, lambda b, t: (0,) * a.ndim)
    gate_b = p['gate_b'].astype(F32)
    pad_row = lambda a: jnp.zeros((1, LANES), F32).at[0, :a.size].set(a)
    consts = [p['conv_w'].astype(F32), pad_row(gate_b[:2 * C_HEADS]), pad_row(gate_b[2 * C_HEADS:]),
              _tri_const(reverse), _tri_const(not reverse)]
    args = [pc, pc, pc] + consts
    in_specs = [main, prev, nxt] + [const2(a) for a in consts]
    pos = (lambda t: nt - 1 - t) if reverse else (lambda t: t)
    io_spec = pl.BlockSpec((None, tb, C_W), lambda b, t: (b, pos(t), 0))
    if final:
        ng = p['norm_g'].astype(F32).reshape(1, C_W)
        args += [ng, h_fwd]
        in_specs += [const2(ng), io_spec]
    blk = lambda: pltpu.VMEM((tb, C_W), F32)
    gate_blk = lambda: pltpu.VMEM((tb, LANES), F32)
    return pl.pallas_call(
        functools.partial(_mlstm_kernel, reverse=reverse, final=final, tb=tb, nt=nt),
        out_shape=jax.ShapeDtypeStruct((bsz, t_len, C_W), BF16 if final else F32),
        grid=(bsz, nt),
        in_specs=in_specs,
        out_specs=io_spec,
        scratch_shapes=[pltpu.VMEM((tb + 2 * HALO, 2 * C_W), F32),
                        pltpu.VMEM((C_HEADS, C_HEAD_DIM, C_HEAD_DIM), F32),
                        pltpu.VMEM((C_HEADS, 1, C_HEAD_DIM), F32), pltpu.VMEM((1, LANES), F32),
                        blk(), blk(), gate_blk(), gate_blk(), blk()],
        compiler_params=pltpu.CompilerParams(dimension_semantics=("parallel", "arbitrary"),
                                             vmem_limit_bytes=VMEM_LIMIT),
        name="mlstm_bwd_final" if final else "mlstm_fwd",
    )(*args)


def mlstm_mix(pc, p, tb):
    h_f = mlstm_pass(pc, p, None, reverse=False, tb=tb)
    return mlstm_pass(pc, p, h_f, reverse=True, tb=tb)


def _pad_cols(w, width):
    return jnp.pad(w, ((0, 0), (0, width - w.shape[1])))


def encoder_trunk(x, p, tb=512):
    bsz, t_len, d = x.shape
    n = bsz * t_len
    x2 = x.reshape(n, d)

    w_in = jnp.concatenate([p['even_w_in'][:, :A_IN], _pad_cols(p['even_w_in'][:, A_IN:], B_IN_PAD)], axis=1)
    pa, pb = norm_matmul(x2, p['norm_mix_g'][0], w_in.astype(BF16), (A_IN, B_IN_PAD))
    ya = rwkv_mix(pa.reshape(bsz, t_len, A_IN), p['rwkv'], tb)
    yb = gdn_mix(pb.reshape(bsz, t_len, B_IN_PAD), p['gdn'], tb)
    w_out = p['even_w_out'].astype(BF16)
    x2 = matmul_residual([ya.reshape(n, A_W), yb.reshape(n, B_W)], [w_out[:A_W], w_out[A_W:]], x2)
    x2 = ffn_residual(x2, p['norm_ffn_g'][0], p['ffn_w1'].astype(BF16), p['ffn_w3'].astype(BF16),
                      p['ffn_w2'].astype(BF16))

    w_odd = p['odd_w_in']
    n_gate = 2 * C_HEADS
    w_in = jnp.concatenate([w_odd[:, :4 * C_W], _pad_cols(w_odd[:, 4 * C_W:4 * C_W + n_gate], LANES),
                            _pad_cols(w_odd[:, 4 * C_W + n_gate:], LANES)], axis=1)
    (pc,) = norm_matmul(x2, p['norm_mix_g'][1], w_in.astype(BF16), (C_IN_PAD,))
    yc = mlstm_mix(pc.reshape(bsz, t_len, C_IN_PAD), p['mlstm'], tb)
    x2 = matmul_residual([yc.reshape(n, C_W)], [p['odd_w_out'].astype(BF16)], x2)
    x2 = moe_residual_final(x2, p['norm_ffn_g'][1], p['moe_router_w'], p['moe_router_b'],
                            p['moe_w1'].astype(BF16), p['moe_w3'].astype(BF16), p['moe_w2'].astype(BF16),
                            p['norm_final_g'])
    return x2.reshape(bsz, t_len, d)


def kernel(x_prompt, x_sample, norm_mix_g, norm_ffn_g, norm_final_g, even_w_in, rwkv_shift_mu, rwkv_w0, rwkv_w_up, rwkv_a0, rwkv_a_up, rwkv_g_up, rwkv_k_k, rwkv_k_a, rwkv_r_k, rwkv_ln_g, rwkv_ln_b, gdn_conv_w, gdn_A_log, gdn_dt_bias, gdn_norm_g, even_w_out, odd_w_in, mlstm_conv_w, mlstm_gate_b, mlstm_norm_g, odd_w_out, ffn_w1, ffn_w3, ffn_w2, moe_router_w, moe_router_b, moe_w1, moe_w3, moe_w2):
    p = dict(
        norm_mix_g=norm_mix_g, norm_ffn_g=norm_ffn_g, norm_final_g=norm_final_g,
        even_w_in=even_w_in[0], even_w_out=even_w_out[0], odd_w_in=odd_w_in[0], odd_w_out=odd_w_out[0],
        ffn_w1=ffn_w1[0], ffn_w3=ffn_w3[0], ffn_w2=ffn_w2[0],
        moe_router_w=moe_router_w[0], moe_router_b=moe_router_b[0],
        moe_w1=moe_w1[0], moe_w3=moe_w3[0], moe_w2=moe_w2[0],
        rwkv=dict(shift_mu=rwkv_shift_mu[0], w0=rwkv_w0[0], w_up=rwkv_w_up[0], a0=rwkv_a0[0], a_up=rwkv_a_up[0],
                  g_up=rwkv_g_up[0], k_k=rwkv_k_k[0], k_a=rwkv_k_a[0], r_k=rwkv_r_k[0], ln_g=rwkv_ln_g[0],
                  ln_b=rwkv_ln_b[0]),
        gdn=dict(conv_w=gdn_conv_w[0], A_log=gdn_A_log[0], dt_bias=gdn_dt_bias[0], norm_g=gdn_norm_g[0]),
        mlstm=dict(conv_w=mlstm_conv_w[0], gate_b=mlstm_gate_b[0], norm_g=mlstm_norm_g[0]),
    )
    return (encoder_trunk(x_prompt, p), encoder_trunk(x_sample, p))
```

```python
import functools

import jax
import jax.numpy as jnp
from jax import lax
from jax.experimental import pallas as pl
from jax.experimental.pallas import tpu as pltpu

F32 = jnp.float32
BF16 = jnp.bfloat16

D_MODEL = 1024
HEAD_DIM = 64
A_HEADS = 8
A_W = 512
W_RANK = 64
ICL_RANK = 64
G_RANK = 128
DECAY_SCALE = 0.606531
GN_EPS = 64e-5
A_IN = 1792
B_HEADS = 8
B_W = 512
C_HEADS = 8
C_HEAD_DIM = 128
C_W = 1024
CONV_W = 5
CHUNK = 64
D_FF = 2816
N_EXPERTS = 8
D_FF_EXPERT = 1408
HALO = 8
LANES = 128
PAIR = 2 * HEAD_DIM
N_PAIRS = A_W // PAIR
CHUNK_GROUP = 2
B_IN_PAD = 4 * B_W + LANES
C_IN_PAD = 4 * C_W + 2 * LANES

VMEM_LIMIT = 56 * 1024 * 1024

NT_DIMS = (((1,), (1,)), ((), ()))
TN_DIMS = (((0,), (0,)), ((), ()))


def _dot(a, b):
    return jnp.dot(a.astype(BF16), b.astype(BF16), preferred_element_type=F32)


def _dot_nt(a, b):
    return lax.dot_general(a.astype(BF16), b.astype(BF16), NT_DIMS, preferred_element_type=F32)


def _dot_tn(a, b):
    return lax.dot_general(a.astype(BF16), b.astype(BF16), TN_DIMS, preferred_element_type=F32)


def _split3(x):
    h = x.astype(BF16)
    r = x - h.astype(F32)
    m = r.astype(BF16)
    l = (r - m.astype(F32)).astype(BF16)
    return h, m, l


def _dot_exact_rhs(x, c):
    h, m, l = _split3(x)
    return (jnp.dot(h, c, preferred_element_type=F32) + jnp.dot(m, c, preferred_element_type=F32)
            + jnp.dot(l, c, preferred_element_type=F32))


def _dot_rhs01(x, c):
    return jnp.dot(x.astype(BF16), c, preferred_element_type=F32)


def _dot_exact_lhs(c, x):
    h, m, l = _split3(x)
    return (jnp.dot(c, h, preferred_element_type=F32) + jnp.dot(c, m, preferred_element_type=F32)
            + jnp.dot(c, l, preferred_element_type=F32))


def _dot_tn_exact_rhs(x, c):
    h, m, l = _split3(x)
    f = lambda p: lax.dot_general(p, c, TN_DIMS, preferred_element_type=F32)
    return f(h) + f(m) + f(l)


def _dot3(a, b):
    ah = a.astype(BF16)
    al = (a - ah.astype(F32)).astype(BF16)
    bh = b.astype(BF16)
    bl = (b - bh.astype(F32)).astype(BF16)
    f = lambda p, q: jnp.dot(p, q, preferred_element_type=F32)
    return f(ah, bh) + f(ah, bl) + f(al, bh)


def _sigmoid(x):
    return 1.0 / (1.0 + jnp.exp(-x))


def _silu(x):
    return x * _sigmoid(x)


def _softplus(x):
    return jnp.maximum(x, 0.0) + jnp.log(1.0 + jnp.exp(-jnp.abs(x)))


def _norm_matmul_kernel(x_ref, g_ref, w_ref, *out_refs, widths, tn):
    x = x_ref[...]
    h = (x * lax.rsqrt(jnp.mean(x * x, axis=-1, keepdims=True) + 1e-6) * g_ref[...]).astype(BF16)
    off = 0
    for o_ref, wd in zip(out_refs, widths):
        for c in range(0, wd, tn):
            cw = min(tn, wd - c)
            o_ref[:, c:c + cw] = jnp.dot(h, w_ref[:, off + c:off + c + cw], preferred_element_type=F32)
        off += wd


def norm_matmul(x2d, g, w, widths, tm=256, tn=512):
    n, d = x2d.shape
    ntot = w.shape[1]
    assert sum(widths) == ntot and n % tm == 0
    return pl.pallas_call(
        functools.partial(_norm_matmul_kernel, widths=widths, tn=tn),
        out_shape=[jax.ShapeDtypeStruct((n, wd), F32) for wd in widths],
        grid=(n // tm,),
        in_specs=[pl.BlockSpec((tm, d), lambda i: (i, 0)),
                  pl.BlockSpec((1, d), lambda i: (0, 0)),
                  pl.BlockSpec((d, ntot), lambda i: (0, 0))],
        out_specs=[pl.BlockSpec((tm, wd), lambda i: (i, 0)) for wd in widths],
        compiler_params=pltpu.CompilerParams(dimension_semantics=("parallel",), vmem_limit_bytes=VMEM_LIMIT),
        name="norm_matmul",
    )(x2d, g.reshape(1, d), w)


def _matmul_res_kernel(*refs, n_in):
    y_refs = refs[:n_in]
    w_refs = refs[n_in:2 * n_in]
    x_ref, o_ref = refs[2 * n_in], refs[2 * n_in + 1]
    acc = x_ref[...]
    for y_ref, w_ref in zip(y_refs, w_refs):
        acc = acc + jnp.dot(y_ref[...], w_ref[...], preferred_element_type=F32)
    o_ref[...] = acc


def matmul_residual(ys, ws, x2d, tm=512):
    n, d = x2d.shape
    assert n % tm == 0
    return pl.pallas_call(
        functools.partial(_matmul_res_kernel, n_in=len(ys)),
        out_shape=jax.ShapeDtypeStruct((n, d), F32),
        grid=(n // tm,),
        in_specs=([pl.BlockSpec((tm, y.shape[1]), lambda i: (i, 0)) for y in ys]
                  + [pl.BlockSpec(w.shape, lambda i: (0, 0)) for w in ws]
                  + [pl.BlockSpec((tm, d), lambda i: (i, 0))]),
        out_specs=pl.BlockSpec((tm, d), lambda i: (i, 0)),
        compiler_params=pltpu.CompilerParams(dimension_semantics=("parallel",), vmem_limit_bytes=VMEM_LIMIT),
        name="matmul_residual",
    )(*ys, *ws, x2d)


def _ff_chunks(width, tf):
    return [(c, min(tf, width - c)) for c in range(0, width, tf)]


def _ffn_kernel(x_ref, g_ref, w1_ref, w3_ref, w2_ref, o_ref, *, tf):
    x = x_ref[...]
    h = (x * lax.rsqrt(jnp.mean(x * x, axis=-1, keepdims=True) + 1e-6) * g_ref[...]).astype(BF16)
    acc = x
    for c, cw in _ff_chunks(D_FF, tf):
        a = jnp.dot(h, w1_ref[:, c:c + cw], preferred_element_type=F32)
        b = jnp.dot(h, w3_ref[:, c:c + cw], preferred_element_type=F32)
        acc = acc + jnp.dot((_silu(a) * b).astype(BF16), w2_ref[c:c + cw, :], preferred_element_type=F32)
    o_ref[...] = acc


def ffn_residual(x2d, g, w1, w3, w2, tm=512, tf=256):
    n, d = x2d.shape
    assert n % tm == 0
    const = lambda shape: pl.BlockSpec(shape, lambda i: (0, 0), pipeline_mode=pl.Buffered(1))
    return pl.pallas_call(
        functools.partial(_ffn_kernel, tf=tf),
        out_shape=jax.ShapeDtypeStruct((n, d), F32),
        grid=(n // tm,),
        in_specs=[pl.BlockSpec((tm, d), lambda i: (i, 0)), const((1, d)),
                  const(w1.shape), const(w3.shape), const(w2.shape)],
        out_specs=pl.BlockSpec((tm, d), lambda i: (i, 0)),
        compiler_params=pltpu.CompilerParams(dimension_semantics=("parallel",), vmem_limit_bytes=VMEM_LIMIT),
        name="ffn_residual",
    )(x2d, g.reshape(1, d), w1, w3, w2)


def _moe_kernel(x_ref, g_ref, rw_ref, rb_ref, w1_ref, w3_ref, w2_ref, gf_ref, o_ref, h_ref, gate_ref, acc_ref, *, tf):
    e = pl.program_id(1)

    @pl.when(e == 0)
    def _():
        x = x_ref[...]
        hf = x * lax.rsqrt(jnp.mean(x * x, axis=-1, keepdims=True) + 1e-6) * g_ref[...]
        h_ref[...] = hf.astype(BF16)
        logits = _dot3(hf, rw_ref[...]) + rb_ref[...]
        lane = lax.broadcasted_iota(jnp.int32, logits.shape, 1)
        m1 = jnp.max(logits, axis=-1, keepdims=True)
        i1 = jnp.min(jnp.where(logits == m1, lane, LANES), axis=-1, keepdims=True)
        rest = jnp.where(lane == i1, -jnp.inf, logits)
        m2 = jnp.max(rest, axis=-1, keepdims=True)
        i2 = jnp.min(jnp.where(rest == m2, lane, LANES), axis=-1, keepdims=True)
        e2 = jnp.exp(m2 - m1)
        den = 1.0 + e2
        gate_ref[...] = jnp.where(lane == i1, 1.0 / den, jnp.where(lane == i2, e2 / den, 0.0))
        acc_ref[...] = x

    lane = lax.broadcasted_iota(jnp.int32, gate_ref.shape, 1)
    ge = jnp.sum(jnp.where(lane == e, gate_ref[...], 0.0), axis=-1, keepdims=True)
    h = h_ref[...]
    y = jnp.zeros(acc_ref.shape, F32)
    for c, cw in _ff_chunks(D_FF_EXPERT, tf):
        a = jnp.dot(h, w1_ref[:, c:c + cw], preferred_element_type=F32)
        b = jnp.dot(h, w3_ref[:, c:c + cw], preferred_element_type=F32)
        y = y + jnp.dot((_silu(a) * b).astype(BF16), w2_ref[c:c + cw, :], preferred_element_type=F32)
    acc_ref[...] += ge * y

    @pl.when(e == N_EXPERTS - 1)
    def _():
        z = acc_ref[...]
        o_ref[...] = z * lax.rsqrt(jnp.mean(z * z, axis=-1, keepdims=True) + 1e-6) * gf_ref[...]


def moe_residual_final(x2d, g, router_w, router_b, w1, w3, w2, g_final, tm=1024, tf=256):
    n, d = x2d.shape
    assert n % tm == 0
    rw = jnp.zeros((d, LANES), F32).at[:, :N_EXPERTS].set(router_w)
    rb = jnp.full((1, LANES), -1e30, F32).at[0, :N_EXPERTS].set(router_b)
    return pl.pallas_call(
        functools.partial(_moe_kernel, tf=tf),
        out_shape=jax.ShapeDtypeStruct((n, d), F32),
        grid=(n // tm, N_EXPERTS),
        in_specs=[pl.BlockSpec((tm, d), lambda i, e: (i, 0)),
                  pl.BlockSpec((1, d), lambda i, e: (0, 0)),
                  pl.BlockSpec((d, LANES), lambda i, e: (0, 0)),
                  pl.BlockSpec((1, LANES), lambda i, e: (0, 0)),
                  pl.BlockSpec((None, d, D_FF_EXPERT), lambda i, e: (e, 0, 0)),
                  pl.BlockSpec((None, d, D_FF_EXPERT), lambda i, e: (e, 0, 0)),
                  pl.BlockSpec((None, D_FF_EXPERT, d), lambda i, e: (e, 0, 0)),
                  pl.BlockSpec((1, d), lambda i, e: (0, 0))],
        out_specs=pl.BlockSpec((tm, d), lambda i, e: (i, 0)),
        scratch_shapes=[pltpu.VMEM((tm, d), BF16), pltpu.VMEM((tm, LANES), F32), pltpu.VMEM((tm, d), F32)],
        compiler_params=pltpu.CompilerParams(dimension_semantics=("parallel", "arbitrary"),
                                             vmem_limit_bytes=VMEM_LIMIT),
        name="moe_residual_final",
    )(x2d, g.reshape(1, d), rw, rb, w1, w3, w2, g_final.reshape(1, d))


def _seq_specs(tb, nt, width, reverse, col_block=0):
    hb = tb // HALO
    nhb = nt * hb
    pos = (lambda t: nt - 1 - t) if reverse else (lambda t: t)
    main = pl.BlockSpec((None, tb, width), lambda b, t: (b, pos(t), col_block))
    prev = pl.BlockSpec((None, HALO, width), lambda b, t: (b, jnp.maximum(pos(t) * hb - 1, 0), col_block))
    nxt = pl.BlockSpec((None, HALO, width), lambda b, t: (b, jnp.minimum((pos(t) + 1) * hb, nhb - 1), col_block))
    return main, prev, nxt


def _fill_extended(xe_ref, x_ref, xp_ref, xn_ref, pos, nt, tb, width=None):
    sl = slice(None) if width is None else slice(0, width)
    xe_ref[0:HALO, :] = jnp.where(pos == 0, 0.0, xp_ref[:, sl])
    xe_ref[HALO:HALO + tb, :] = x_ref[:, sl]
    xe_ref[HALO + tb:2 * HALO + tb, :] = jnp.where(pos == nt - 1, 0.0, xn_ref[:, sl])


def _seg_mats():
    r = lax.broadcasted_iota(jnp.int32, (A_W, A_W), 0) // HEAD_DIM
    c = lax.broadcasted_iota(jnp.int32, (A_W, A_W), 1) // HEAD_DIM
    return (r == c).astype(BF16)


def _tri_const(reverse, reps=1):
    r = jnp.arange(CHUNK)[:, None]
    c = jnp.arange(CHUNK)[None, :]
    m = ((c >= r) if reverse else (c <= r)).astype(BF16)
    return jnp.tile(m, (1, reps))


def _pair_masks(reverse):
    row = lax.broadcasted_iota(jnp.int32, (CHUNK, PAIR), 0)
    lane = lax.broadcasted_iota(jnp.int32, (CHUNK, PAIR), 1)
    col = lane % HEAD_DIM
    strict = (col > row) if reverse else (col < row)
    incl = (col >= row) if reverse else (col <= row)
    eye = (col == row).astype(F32)
    r2 = lax.broadcasted_iota(jnp.int32, (PAIR, PAIR), 0) // HEAD_DIM
    c2 = lax.broadcasted_iota(jnp.int32, (PAIR, PAIR), 1) // HEAD_DIM
    same_head = r2 == c2
    return strict, incl, eye, same_head.astype(BF16), lane < HEAD_DIM, same_head


def _bd(x2, bdm):
    xb = x2.astype(BF16)
    return jnp.concatenate([xb, xb], axis=0) * bdm


def _mm(a, b):
    return jnp.dot(a.astype(BF16), b, preferred_element_type=F32)


def _tri_inverse_levels(n2s, eye2, bdm, out):
    ps = [_mm(n, _bd(n, bdm)) for n in n2s]
    ts = [eye2 + n for n in n2s]
    yield
    for i in range(5):
        if i < 4:
            outs = [_mm(p, jnp.concatenate([_bd(t, bdm), _bd(p, bdm)], axis=1)) for t, p in zip(ts, ps)]
            ps = [o[:, PAIR:] for o in outs]
        else:
            outs = [_mm(p, _bd(t, bdm)) for t, p in zip(ts, ps)]
        ts = [t + o[:, 0:PAIR] for t, o in zip(ts, outs)]
        yield
    out.extend(ts)


def _run_interleaved(*gens):
    live = [g for g in gens if g is not None]
    while live:
        for g in list(live):
            try:
                next(g)
            except StopIteration:
                live.remove(g)


def _software_pipeline(n_groups, phase_a, chain):
    _run_interleaved(phase_a(0))
    for g in range(n_groups):
        _run_interleaved(chain(g), phase_a(g + 1) if g + 1 < n_groups else None)


def _rwkv_kernel(*refs, reverse, final, tb, nt, group):
    if final:
        (x_ref, xp_ref, xn_ref, mu_ref, kk_ref, ka_ref, w0_ref, wup_ref, a0_ref, aup_ref, seg_ref, tri_ref,
         gup_ref, rk_ref, lng_ref, lnb_ref, yf_ref, o_ref,
         xe_ref, s_ref, r_s, k_s, v_s, a_s, b_s, lw_s, y_s) = refs
    else:
        (x_ref, xp_ref, xn_ref, mu_ref, kk_ref, ka_ref, w0_ref, wup_ref, a0_ref, aup_ref, seg_ref, tri_ref,
         o_ref, xe_ref, s_ref, r_s, k_s, v_s, a_s, b_s, lw_s, y_s) = refs
    d = 1 if reverse else 0
    t = pl.program_id(1)
    pos = (nt - 1 - t) if reverse else t
    nc = tb // CHUNK

    @pl.when(t == 0)
    def _():
        s_ref[...] = jnp.zeros(s_ref.shape, F32)

    _fill_extended(xe_ref, x_ref, xp_ref, xn_ref, pos, nt, tb)
    x = x_ref[...]
    nb = 0.5 * (xe_ref[pl.ds(HALO - 1, tb), :] + xe_ref[pl.ds(HALO + 1, tb), :])
    pa = x + mu_ref[...] * (nb - x)
    r = pa[:, 0:A_W]
    k = pa[:, A_W:2 * A_W]
    v = pa[:, 2 * A_W:3 * A_W]
    wd = jnp.tanh(pa[:, 3 * A_W:3 * A_W + W_RANK])
    ad = pa[:, 3 * A_W + W_RANK:3 * A_W + W_RANK + ICL_RANK]
    seg = seg_ref[...]
    kq = k * kk_ref[...]
    kk = kq * lax.rsqrt(_dot_rhs01(kq * kq, seg) + 1e-6)
    lw = -DECAY_SCALE * _sigmoid(w0_ref[d:d + 1, :] + _dot(wd, wup_ref[d]))
    icl = _sigmoid(a0_ref[d:d + 1, :] + _dot(ad, aup_ref[d]))
    ka = ka_ref[...]
    r_s[...] = r
    k_s[...] = k * (1.0 + (icl - 1.0) * ka)
    v_s[...] = v
    a_s[...] = -kk
    b_s[...] = kk * icl
    lw_s[...] = lw

    strict, incl, eye2, bdm, _, same_head = _pair_masks(reverse)
    tri = tri_ref[...]

    groups = [list(range(nc))[j:j + group] for j in range(0, nc, group)]
    if reverse:
        groups = [[nc - 1 - c for c in g] for g in groups]
    staged = {}

    def phase_a(g):
        probs = []
        for c in groups[g]:
            rows = slice(c * CHUNK, (c + 1) * CHUNK)
            lwc = lw_s[rows, :]
            cum = _dot_exact_lhs(tri, lwc)
            last = cum[0:1, :] if reverse else cum[CHUNK - 1:CHUNK, :]
            rt = r_s[rows, :] * jnp.exp(cum)
            at = a_s[rows, :] * jnp.exp(cum - lwc)
            einv = jnp.exp(-cum)
            bc = b_s[rows, :]
            kc = k_s[rows, :]
            bt = bc * einv
            kt = kc * einv
            eend = jnp.exp(last - cum)
            bh = bc * eend
            kh = kc * eend
            gl = jnp.exp(last)
            vc = v_s[rows, :]
            for p in range(N_PAIRS):
                sl = slice(p * PAIR, (p + 1) * PAIR)
                probs.append(dict(c=c, p=p, at=at[:, sl], rt=rt[:, sl], bt=bt[:, sl], kt=kt[:, sl],
                                  bk=jnp.concatenate([bh[:, sl], kh[:, sl]], axis=0).astype(BF16),
                                  gl=gl[:, sl], v=vc[:, sl]))
        yield
        for q in probs:
            xm = jnp.concatenate([q['at'], q['rt']], axis=0).astype(BF16)
            wm = jnp.concatenate([_bd(q.pop('bt'), bdm), _bd(q.pop('kt'), bdm)], axis=0)
            pm = lax.dot_general(xm, wm, NT_DIMS, preferred_element_type=F32)
            q['a_ab'] = jnp.where(strict, pm[0:CHUNK, 0:PAIR], 0.0)
            q['a_kr'] = jnp.concatenate([jnp.where(strict, pm[0:CHUNK, PAIR:], 0.0),
                                         jnp.where(incl, pm[CHUNK:, PAIR:], 0.0)], axis=0).astype(BF16)
            q['a_rb'] = jnp.where(incl, pm[CHUNK:, 0:PAIR], 0.0).astype(BF16)
        yield
        tinvs = []
        yield from _tri_inverse_levels([q.pop('a_ab') for q in probs], eye2, bdm, tinvs)
        for q in probs:
            q['wy'] = _mm(q.pop('a_kr'), _bd(q['v'], bdm))
        yield
        for q, tinv in zip(probs, tinvs):
            wy = q.pop('wy')
            ua = _mm(tinv, jnp.concatenate([_bd(wy[0:CHUNK], bdm), _bd(q.pop('at'), bdm)], axis=1))
            q['u0'] = ua[:, 0:PAIR]
            q['ar'] = jnp.concatenate([ua[:, PAIR:], q.pop('rt')], axis=0).astype(BF16)
            q['y0'] = wy[CHUNK:]
        yield
        staged[g] = probs

    def chain(g):
        probs = staged.pop(g)
        for j in range(0, len(probs), N_PAIRS):
            cp = probs[j:j + N_PAIRS]
            ss = [s_ref[q['p']] for q in cp]
            uys = [lax.dot_general(q['ar'], s.astype(BF16), NT_DIMS, preferred_element_type=F32)
                   for q, s in zip(cp, ss)]
            yield
            ys = []
            for q, s, uy in zip(cp, ss, uys):
                u = uy[0:CHUNK] + q['u0']
                ys.append(uy[CHUNK:] + _mm(q['a_rb'], _bd(u, bdm)) + q['y0'])
                upd = lax.dot_general(jnp.concatenate([u, q['v']], axis=0).astype(BF16), q['bk'], TN_DIMS,
                                      preferred_element_type=F32)
                s_ref[q['p']] = s * q['gl'] + jnp.where(same_head, upd, 0.0)
            c = cp[0]['c']
            y_s[c * CHUNK:(c + 1) * CHUNK, :] = jnp.concatenate(ys, axis=1)
            yield

    _software_pipeline(len(groups), phase_a, chain)

    if not final:
        o_ref[...] = y_s[...]
    else:
        y = yf_ref[...] + y_s[...]
        seg_mean = lambda z: _dot_rhs01(z, seg) * (1.0 / HEAD_DIM)
        mu = seg_mean(y)
        yc = y - mu
        var = seg_mean(yc * yc)
        yn = yc * lax.rsqrt(var + GN_EPS) * lng_ref[...] + lnb_ref[...]
        o = 1 - d
        icl_o = _sigmoid(a0_ref[o:o + 1, :] + _dot(ad, aup_ref[o]))
        ksum = k * (2.0 + (icl + icl_o - 2.0) * ka)
        bonus = _dot_rhs01(r * ksum * rk_ref[...], seg) * v
        gate = _dot(_sigmoid(pa[:, 3 * A_W + W_RANK + ICL_RANK:]), gup_ref[...])
        o_ref[...] = ((yn + bonus) * gate).astype(o_ref.dtype)


def rwkv_pass(pa, p, y_fwd, *, reverse, tb):
    bsz, t_len, _ = pa.shape
    nt = t_len // tb
    final = y_fwd is not None
    main, prev, nxt = _seq_specs(tb, nt, A_IN, reverse)
    const2 = lambda a: pl.BlockSpec(a.shape, lambda b, t: (0,) * a.ndim)
    row = lambda a: a.reshape(1, -1).astype(F32)
    consts = [row(p['shift_mu']), row(p['k_k']), row(p['k_a']), p['w0'].astype(F32), p['w_up'].astype(BF16),
              p['a0'].astype(F32), p['a_up'].astype(BF16), _seg_mats(), _tri_const(reverse)]
    args = [pa, pa, pa] + consts
    in_specs = [main, prev, nxt] + [const2(a) for a in consts]
    pos = (lambda t: nt - 1 - t) if reverse else (lambda t: t)
    io_spec = pl.BlockSpec((None, tb, A_W), lambda b, t: (b, pos(t), 0))
    if final:
        extra = [p['g_up'].astype(BF16), row(p['r_k']), row(p['ln_g']), row(p['ln_b'])]
        args += extra + [y_fwd]
        in_specs += [const2(a) for a in extra] + [io_spec]
    blk = lambda: pltpu.VMEM((tb, A_W), F32)
    return pl.pallas_call(
        functools.partial(_rwkv_kernel, reverse=reverse, final=final, tb=tb, nt=nt, group=CHUNK_GROUP),
        out_shape=jax.ShapeDtypeStruct((bsz, t_len, A_W), BF16 if final else F32),
        grid=(bsz, nt),
        in_specs=in_specs,
        out_specs=io_spec,
        scratch_shapes=[pltpu.VMEM((tb + 2 * HALO, A_IN), F32), pltpu.VMEM((N_PAIRS, PAIR, PAIR), F32),
                        blk(), blk(), blk(), blk(), blk(), blk(), blk()],
        compiler_params=pltpu.CompilerParams(dimension_semantics=("parallel", "arbitrary"),
                                             vmem_limit_bytes=VMEM_LIMIT),
        name="rwkv_bwd_final" if final else "rwkv_fwd",
    )(*args)


def rwkv_mix(pa, p, tb):
    y_f = rwkv_pass(pa, p, None, reverse=False, tb=tb)
    return rwkv_pass(pa, p, y_f, reverse=True, tb=tb)


def _gdn_kernel(*refs, reverse, final, tb, nt, group):
    if final:
        (qkv_ref, z_ref, gl_ref, alog_ref, dtb_ref, seg_ref, tri_ref, trit_ref, exg_ref, exb_ref,
         ng_ref, of_ref, o_ref, s_ref, q_s, k_s, v_s, g_s, bx_s, y_s) = refs
    else:
        (x_ref, xp_ref, xn_ref, cw_ref, alog_ref, dtb_ref, seg_ref, tri_ref, trit_ref, exg_ref, exb_ref,
         o_ref, qkv_out_ref, xe_ref, s_ref, q_s, k_s, v_s, g_s, bx_s, y_s) = refs
    d = 1 if reverse else 0
    t = pl.program_id(1)
    pos = (nt - 1 - t) if reverse else t
    nc = tb // CHUNK
    qkv_w = 3 * B_W

    @pl.when(t == 0)
    def _():
        s_ref[...] = jnp.zeros(s_ref.shape, F32)

    seg = seg_ref[...]
    if final:
        q_s[...] = qkv_ref[:, 0:B_W].astype(F32)
        k_s[...] = qkv_ref[:, B_W:2 * B_W].astype(F32)
        v_s[...] = qkv_ref[:, 2 * B_W:].astype(F32)
        gl = gl_ref[...]
    else:
        _fill_extended(xe_ref, x_ref, xp_ref, xn_ref, pos, nt, tb, width=qkv_w)
        conv = xe_ref[pl.ds(HALO - 2, tb), :] * cw_ref[0:1, :]
        for j in range(1, CONV_W):
            conv = conv + xe_ref[pl.ds(HALO - 2 + j, tb), :] * cw_ref[j:j + 1, :]
        qkv = _silu(conv)
        q = qkv[:, 0:B_W]
        k = qkv[:, B_W:2 * B_W]
        qn = q * lax.rsqrt(_dot_rhs01(q * q, seg) + 1e-6) * (HEAD_DIM ** -0.5)
        kn = k * lax.rsqrt(_dot_rhs01(k * k, seg) + 1e-6)
        q_s[...] = qn
        k_s[...] = kn
        v_s[...] = qkv[:, 2 * B_W:]
        qkv_out_ref[...] = jnp.concatenate([qn, kn, qkv[:, 2 * B_W:]], axis=1).astype(BF16)
        gl = x_ref[:, 4 * B_W:4 * B_W + LANES]
    g_s[...] = -jnp.exp(alog_ref[...]) * _softplus(gl + dtb_ref[...])
    bx_s[...] = _dot_rhs01(_sigmoid(gl), exb_ref[...])

    strict, incl, eye2, bdm, left, same_head = _pair_masks(reverse)
    tri = tri_ref[...]
    trit2 = trit_ref[...]
    exg = exg_ref[...]

    groups = [list(range(nc))[j:j + group] for j in range(0, nc, group)]
    if reverse:
        groups = [[nc - 1 - c for c in g] for g in groups]
    staged = {}

    def phase_a(g):
        probs = []
        for c in groups[g]:
            rows = slice(c * CHUNK, (c + 1) * CHUNK)
            gch = g_s[rows, :]
            gc = _dot_exact_lhs(tri, gch)
            gct = _dot_tn_exact_rhs(gch, trit2)
            last = gc[0:1, :] if reverse else gc[CHUNK - 1:CHUNK, :]
            eg = _dot_rhs01(jnp.exp(gc), exg)
            eend = _dot_rhs01(jnp.exp(last - gc), exg)
            glast = _dot_exact_rhs(jnp.exp(last), exg)
            qc = q_s[rows, :]
            kc = k_s[rows, :]
            vc = v_s[rows, :]
            beta = bx_s[rows, :]
            kb = kc * beta
            vb = vc * beta
            kbe = kb * eg
            qd = qc * eg
            kd = kc * eend
            for p in range(N_PAIRS):
                sl = slice(p * PAIR, (p + 1) * PAIR)
                l0 = d * B_HEADS + 2 * p
                diff = (jnp.where(left, gc[:, l0:l0 + 1], gc[:, l0 + 1:l0 + 2])
                        - jnp.where(left[0:1], gct[l0:l0 + 1, :], gct[l0 + 1:l0 + 2, :]))
                probs.append(dict(c=c, p=p, decay=jnp.exp(jnp.where(incl, diff, 0.0)),
                                  kq=jnp.concatenate([kb[:, sl], qc[:, sl]], axis=0).astype(BF16), k=kc[:, sl],
                                  vb=vb[:, sl], kbe=kbe[:, sl], qd=qd[:, sl], kd=kd[:, sl].astype(BF16),
                                  gl=glast[:, sl]))
        yield
        for q in probs:
            pm = lax.dot_general(q.pop('kq'), _bd(q.pop('k'), bdm), NT_DIMS, preferred_element_type=F32)
            decay = q.pop('decay')
            q['n'] = jnp.where(strict, -pm[0:CHUNK] * decay, 0.0)
            q['qk'] = jnp.where(incl, pm[CHUNK:] * decay, 0.0).astype(BF16)
        yield
        tinvs = []
        yield from _tri_inverse_levels([q.pop('n') for q in probs], eye2, bdm, tinvs)
        for q, tinv in zip(probs, tinvs):
            sol = _mm(tinv, jnp.concatenate([_bd(q.pop('vb'), bdm), _bd(q.pop('kbe'), bdm)], axis=1))
            q['u'] = sol[:, 0:PAIR]
            q['wq'] = jnp.concatenate([sol[:, PAIR:], q.pop('qd')], axis=0).astype(BF16)
        yield
        staged[g] = probs

    def chain(g):
        probs = staged.pop(g)
        for j in range(0, len(probs), N_PAIRS):
            cp = probs[j:j + N_PAIRS]
            ss = [s_ref[q['p']] for q in cp]
            wss = [jnp.dot(q['wq'], s.astype(BF16), preferred_element_type=F32) for q, s in zip(cp, ss)]
            yield
            ys = []
            for q, s, ws in zip(cp, ss, wss):
                v_new = q['u'] - ws[0:CHUNK]
                ys.append(ws[CHUNK:] + _mm(q['qk'], _bd(v_new, bdm)))
                upd = lax.dot_general(q['kd'], v_new.astype(BF16), TN_DIMS, preferred_element_type=F32)
                s_ref[q['p']] = s * q['gl'] + jnp.where(same_head, upd, 0.0)
            c = cp[0]['c']
            y_s[c * CHUNK:(c + 1) * CHUNK, :] = jnp.concatenate(ys, axis=1)
            yield

    _software_pipeline(len(groups), phase_a, chain)

    if not final:
        o_ref[...] = y_s[...]
    else:
        o = of_ref[...] + y_s[...]
        ms = _dot_rhs01(o * o, seg) * (1.0 / HEAD_DIM)
        o_ref[...] = (o * lax.rsqrt(ms + 1e-6) * ng_ref[...] * _silu(z_ref[...])).astype(o_ref.dtype)


def _expand_mat(first_lane):
    r = jnp.arange(LANES)[:, None]
    c = jnp.arange(B_W)[None, :] // HEAD_DIM
    return (r == c + first_lane).astype(BF16)


def gdn_pass(pb, p, fwd_out, *, reverse, tb):
    bsz, t_len, _ = pb.shape
    nt = t_len // tb
    final = fwd_out is not None
    d = 1 if reverse else 0
    const2 = lambda a: pl.BlockSpec(a.shape, lambda b, t: (0,) * a.ndim)
    pad_row = lambda a: jnp.zeros((1, LANES), F32).at[0, :a.size].set(a.reshape(-1))
    consts = [pad_row(p['A_log']), pad_row(p['dt_bias']), _seg_mats(),
              _tri_const(reverse), _tri_const(not reverse, reps=2), _expand_mat(d * B_HEADS),
              _expand_mat(2 * B_HEADS + d * B_HEADS)]
    pos = (lambda t: nt - 1 - t) if reverse else (lambda t: t)
    blk_spec = lambda width, col_block: pl.BlockSpec((None, tb, width), lambda b, t: (b, pos(t), col_block))
    io_spec = blk_spec(B_W, 0)
    blk = lambda: pltpu.VMEM((tb, B_W), F32)
    scratch = [pltpu.VMEM((N_PAIRS, PAIR, PAIR), F32), blk(), blk(), blk(), pltpu.VMEM((tb, LANES), F32),
               blk(), blk()]
    if final:
        o_fwd, qkv = fwd_out
        ng = jnp.tile(p['norm_g'].astype(F32), B_HEADS).reshape(1, B_W)
        args = [qkv, pb, pb] + consts + [ng, o_fwd]
        in_specs = ([blk_spec(3 * B_W, 0), blk_spec(B_W, 3), blk_spec(LANES, 4 * B_W // LANES)]
                    + [const2(a) for a in consts] + [const2(ng), io_spec])
        out_shape = jax.ShapeDtypeStruct((bsz, t_len, B_W), BF16)
        out_specs = io_spec
    else:
        main, prev, nxt = _seq_specs(tb, nt, B_IN_PAD, reverse)
        consts = [p['conv_w'].astype(F32)] + consts
        args = [pb, pb, pb] + consts
        in_specs = [main, prev, nxt] + [const2(a) for a in consts]
        out_shape = [jax.ShapeDtypeStruct((bsz, t_len, B_W), F32), jax.ShapeDtypeStruct((bsz, t_len, 3 * B_W), BF16)]
        out_specs = [io_spec, blk_spec(3 * B_W, 0)]
        scratch = [pltpu.VMEM((tb + 2 * HALO, 3 * B_W), F32)] + scratch
    return pl.pallas_call(
        functools.partial(_gdn_kernel, reverse=reverse, final=final, tb=tb, nt=nt, group=CHUNK_GROUP),
        out_shape=out_shape,
        grid=(bsz, nt),
        in_specs=in_specs,
        out_specs=out_specs,
        scratch_shapes=scratch,
        compiler_params=pltpu.CompilerParams(dimension_semantics=("parallel", "arbitrary"),
                                             vmem_limit_bytes=VMEM_LIMIT),
        name="gdn_bwd_final" if final else "gdn_fwd",
    )(*args)


def gdn_mix(pb, p, tb):
    fwd_out = gdn_pass(pb, p, None, reverse=False, tb=tb)
    return gdn_pass(pb, p, fwd_out, reverse=True, tb=tb)


def _mlstm_kernel(*refs, reverse, final, tb, nt):
    if final:
        (qk_ref, v_ref, og_ref, gt_ref, bi_ref, bf_ref, tri_ref, ng_ref, hf_ref, o_ref,
         c_ref, n_ref, m_ref, q_s, k_s, li_s, lf_s, y_s) = refs
    else:
        (x_ref, xp_ref, xn_ref, cw_ref, bi_ref, bf_ref, tri_ref, o_ref, qk_out_ref,
         xe_ref, c_ref, n_ref, m_ref, q_s, k_s, li_s, lf_s, y_s) = refs
    d = 1 if reverse else 0
    t = pl.program_id(1)
    pos = (nt - 1 - t) if reverse else t
    nc = tb // CHUNK
    qk_w = 2 * C_W
    dh = C_HEAD_DIM

    @pl.when(t == 0)
    def _():
        c_ref[...] = jnp.zeros(c_ref.shape, F32)
        n_ref[...] = jnp.zeros(n_ref.shape, F32)
        m_ref[...] = jnp.zeros(m_ref.shape, F32)

    if final:
        q_s[...] = qk_ref[:, 0:C_W].astype(F32)
        k_s[...] = qk_ref[:, C_W:].astype(F32)
        gates = gt_ref[...]
        v_src, v_col = v_ref, 0
    else:
        _fill_extended(xe_ref, x_ref, xp_ref, xn_ref, pos, nt, tb, width=qk_w)
        conv = xe_ref[pl.ds(HALO - 2, tb), :] * cw_ref[0:1, :]
        for j in range(1, CONV_W):
            conv = conv + xe_ref[pl.ds(HALO - 2 + j, tb), :] * cw_ref[j:j + 1, :]
        qk = _silu(conv)
        qs = qk[:, 0:C_W] * (dh ** -0.5)
        q_s[...] = qs
        k_s[...] = qk[:, C_W:]
        qk_out_ref[...] = jnp.concatenate([qs, qk[:, C_W:]], axis=1).astype(BF16)
        gates = x_ref[:, 4 * C_W:4 * C_W + 2 * LANES]
        v_src, v_col = x_ref, 2 * C_W
    li_s[...] = gates[:, 0:LANES] + bi_ref[...]
    lf_s[...] = -_softplus(-(gates[:, LANES:] + bf_ref[...]))

    row = lax.broadcasted_iota(jnp.int32, (CHUNK, CHUNK), 0)
    col = lax.broadcasted_iota(jnp.int32, (CHUNK, CHUNK), 1)
    incl = (col >= row) if reverse else (col <= row)
    eye_b = (col == row).astype(BF16)
    tri = tri_ref[...]

    row128 = lax.broadcasted_iota(jnp.int32, (CHUNK, LANES), 0)

    def running_max(x):
        s = 1
        while s < CHUNK:
            if reverse:
                shifted = jnp.where(row128 >= CHUNK - s, -jnp.inf, pltpu.roll(x, CHUNK - s, axis=0))
            else:
                shifted = jnp.where(row128 < s, -jnp.inf, pltpu.roll(x, s, axis=0))
            x = jnp.maximum(x, shifted)
            s *= 2
        return x

    state = dict(c=[c_ref[h] for h in range(C_HEADS)], n=[n_ref[h] for h in range(C_HEADS)], m=m_ref[...])

    def chunk_steps(c):
        rows = slice(c * CHUNK, (c + 1) * CHUNK)
        lfc = lf_s[rows, :]
        lic = li_s[rows, :]
        bcum = _dot_exact_lhs(tri, lfc)
        last = bcum[0:1, :] if reverse else bcum[CHUNK - 1:CHUNK, :]
        gkey = lic - bcum
        gkey_t = _dot_tn_exact_rhs(gkey, eye_b)
        m = state['m']
        w_end = last + gkey
        m_new = jnp.maximum(last + m, jnp.max(w_end, axis=0, keepdims=True))
        s_old = jnp.exp(last + m - m_new)
        sk_scale = jnp.exp(w_end - m_new)
        a_inter = bcum + m
        state['m'] = m_new
        m_row = jnp.maximum(a_inter, bcum + running_max(gkey))
        r_row = bcum - m_row
        s_inter = jnp.exp(a_inter - m_row)
        e_mrow = jnp.exp(-m_row)
        qc = q_s[rows, :]
        kc = k_s[rows, :]
        vc = v_src[rows, v_col:v_col + C_W]
        hd = []
        for h in range(C_HEADS):
            sl = slice(h * dh, (h + 1) * dh)
            qh, kh = qc[:, sl].astype(BF16), kc[:, sl]
            hd.append(dict(h=h, l=d * C_HEADS + h, sl=sl, qh=qh, kh=kh, vh=vc[:, sl].astype(BF16),
                           qk=lax.dot_general(qh, kh.astype(BF16), NT_DIMS, preferred_element_type=F32)))
        for e in hd:
            cm, nm = state['c'][e['h']], state['n'][e['h']]
            e['qc'] = jnp.dot(e.pop('qh'), cm.astype(BF16), preferred_element_type=F32)
            e['qn'] = jnp.sum(qc[:, e['sl']] * nm, axis=-1, keepdims=True)
        for e in hd:
            h, l = e['h'], e['l']
            s_k = sk_scale[:, l:l + 1] * e.pop('kh')
            kv = lax.dot_general(s_k.astype(BF16), e['vh'], TN_DIMS, preferred_element_type=F32)
            state['c'][h] = s_old[:, l:l + 1] * state['c'][h] + kv
            state['n'][h] = s_old[:, l:l + 1] * state['n'][h] + jnp.sum(s_k, axis=0, keepdims=True)
        yield
        ys = []
        for e in hd:
            l = e['l']
            expo = jnp.where(incl, r_row[:, l:l + 1] + gkey_t[l:l + 1, :], -jnp.inf)
            pm = e['qk'] * jnp.exp(expo)
            num = s_inter[:, l:l + 1] * e['qc'] + _mm(pm, e['vh'])
            den = s_inter[:, l:l + 1] * e['qn'] + jnp.sum(pm, axis=-1, keepdims=True)
            ys.append(num / jnp.maximum(jnp.abs(den), e_mrow[:, l:l + 1]))
        y_s[rows, :] = jnp.concatenate(ys, axis=1)
        yield

    order = [(nc - 1 - i) if reverse else i for i in range(nc)]
    prev = None
    for c in order:
        cur = chunk_steps(c)
        next(cur)
        if prev is not None:
            next(prev)
        prev = cur
    next(prev)
    for h in range(C_HEADS):
        c_ref[h] = state['c'][h]
        n_ref[h] = state['n'][h]
    m_ref[...] = state['m']

    if not final:
        o_ref[...] = y_s[...]
    else:
        hs = hf_ref[...] + y_s[...]
        outs = []
        for h in range(C_HEADS):
            sl = slice(h * dh, (h + 1) * dh)
            hh = hs[:, sl]
            outs.append(hh * lax.rsqrt(jnp.mean(hh * hh, axis=-1, keepdims=True) + 1e-6))
        hn = jnp.concatenate(outs, axis=1) * ng_ref[...]
        o_ref[...] = (_sigmoid(og_ref[...]) * hn).astype(o_ref.dtype)


def mlstm_pass(pc, p, fwd_out, *, reverse, tb):
    bsz, t_len, _ = pc.shape
    nt = t_len // tb
    final = fwd_out is not None
    const2 = lambda a: pl.BlockSpec(a.shape, lambda b, t: (0,) * a.ndim)
    gate_b = p['gate_b'].astype(F32)
    pad_row = lambda a: jnp.zeros((1, LANES), F32).at[0, :a.size].set(a)
    consts = [pad_row(gate_b[:2 * C_HEADS]), pad_row(gate_b[2 * C_HEADS:]), _tri_const(reverse)]
    pos = (lambda t: nt - 1 - t) if reverse else (lambda t: t)
    blk_spec = lambda width, col_block: pl.BlockSpec((None, tb, width), lambda b, t: (b, pos(t), col_block))
    io_spec = blk_spec(C_W, 0)
    blk = lambda: pltpu.VMEM((tb, C_W), F32)
    gate_blk = lambda: pltpu.VMEM((tb, LANES), F32)
    scratch = [pltpu.VMEM((C_HEADS, C_HEAD_DIM, C_HEAD_DIM), F32), pltpu.VMEM((C_HEADS, 1, C_HEAD_DIM), F32),
               pltpu.VMEM((1, LANES), F32), blk(), blk(), gate_blk(), gate_blk(), blk()]
    if final:
        h_fwd, qk = fwd_out
        ng = p['norm_g'].astype(F32).reshape(1, C_W)
        args = [qk, pc, pc, pc] + consts + [ng, h_fwd]
        in_specs = ([blk_spec(2 * C_W, 0), blk_spec(C_W, 2), blk_spec(C_W, 3), blk_spec(2 * LANES, 2 * C_W // LANES)]
                    + [const2(a) for a in consts] + [const2(ng), io_spec])
        out_shape = jax.ShapeDtypeStruct((bsz, t_len, C_W), BF16)
        out_specs = io_spec
    else:
        main, prev, nxt = _seq_specs(tb, nt, C_IN_PAD, reverse)
        consts = [p['conv_w'].astype(F32)] + consts
        args = [pc, pc, pc] + consts
        in_specs = [main, prev, nxt] + [const2(a) for a in consts]
        out_shape = [jax.ShapeDtypeStruct((bsz, t_len, C_W), F32), jax.ShapeDtypeStruct((bsz, t_len, 2 * C_W), BF16)]
        out_specs = [io_spec, blk_spec(2 * C_W, 0)]
        scratch = [pltpu.VMEM((tb + 2 * HALO, 2 * C_W), F32)] + scratch
    return pl.pallas_call(
        functools.partial(_mlstm_kernel, reverse=reverse, final=final, tb=tb, nt=nt),
        out_shape=out_shape,
        grid=(bsz, nt),
        in_specs=in_specs,
        out_specs=out_specs,
        scratch_shapes=scratch,
        compiler_params=pltpu.CompilerParams(dimension_semantics=("parallel", "arbitrary"),
                                             vmem_limit_bytes=VMEM_LIMIT),
        name="mlstm_bwd_final" if final else "mlstm_fwd",
    )(*args)


def mlstm_mix(pc, p, tb):
    fwd_out = mlstm_pass(pc, p, None, reverse=False, tb=tb)
    return mlstm_pass(pc, p, fwd_out, reverse=True, tb=tb)


def _pad_cols(w, width):
    return jnp.pad(w, ((0, 0), (0, width - w.shape[1])))


def encoder_trunk(x, p, tb=512):
    bsz, t_len, d = x.shape
    n = bsz * t_len
    x2 = x.reshape(n, d)

    w_in = jnp.concatenate([p['even_w_in'][:, :A_IN], _pad_cols(p['even_w_in'][:, A_IN:], B_IN_PAD)], axis=1)
    pa, pb = norm_matmul(x2, p['norm_mix_g'][0], w_in.astype(BF16), (A_IN, B_IN_PAD))
    ya = rwkv_mix(pa.reshape(bsz, t_len, A_IN), p['rwkv'], tb)
    yb = gdn_mix(pb.reshape(bsz, t_len, B_IN_PAD), p['gdn'], tb)
    w_out = p['even_w_out'].astype(BF16)
    x2 = matmul_residual([ya.reshape(n, A_W), yb.reshape(n, B_W)], [w_out[:A_W], w_out[A_W:]], x2)
    x2 = ffn_residual(x2, p['norm_ffn_g'][0], p['ffn_w1'].astype(BF16), p['ffn_w3'].astype(BF16),
                      p['ffn_w2'].astype(BF16))

    w_odd = p['odd_w_in']
    n_gate = 2 * C_HEADS
    w_in = jnp.concatenate([w_odd[:, :4 * C_W], _pad_cols(w_odd[:, 4 * C_W:4 * C_W + n_gate], LANES),
                            _pad_cols(w_odd[:, 4 * C_W + n_gate:], LANES)], axis=1)
    (pc,) = norm_matmul(x2, p['norm_mix_g'][1], w_in.astype(BF16), (C_IN_PAD,))
    yc = mlstm_mix(pc.reshape(bsz, t_len, C_IN_PAD), p['mlstm'], tb)
    x2 = matmul_residual([yc.reshape(n, C_W)], [p['odd_w_out'].astype(BF16)], x2)
    x2 = moe_residual_final(x2, p['norm_ffn_g'][1], p['moe_router_w'], p['moe_router_b'],
                            p['moe_w1'].astype(BF16), p['moe_w3'].astype(BF16), p['moe_w2'].astype(BF16),
                            p['norm_final_g'])
    return x2.reshape(bsz, t_len, d)


def kernel(x_prompt, x_sample, norm_mix_g, norm_ffn_g, norm_final_g, even_w_in, rwkv_shift_mu, rwkv_w0, rwkv_w_up, rwkv_a0, rwkv_a_up, rwkv_g_up, rwkv_k_k, rwkv_k_a, rwkv_r_k, rwkv_ln_g, rwkv_ln_b, gdn_conv_w, gdn_A_log, gdn_dt_bias, gdn_norm_g, even_w_out, odd_w_in, mlstm_conv_w, mlstm_gate_b, mlstm_norm_g, odd_w_out, ffn_w1, ffn_w3, ffn_w2, moe_router_w, moe_router_b, moe_w1, moe_w3, moe_w2):
    p = dict(
        norm_mix_g=norm_mix_g, norm_ffn_g=norm_ffn_g, norm_final_g=norm_final_g,
        even_w_in=even_w_in[0], even_w_out=even_w_out[0], odd_w_in=odd_w_in[0], odd_w_out=odd_w_out[0],
        ffn_w1=ffn_w1[0], ffn_w3=ffn_w3[0], ffn_w2=ffn_w2[0],
        moe_router_w=moe_router_w[0], moe_router_b=moe_router_b[0],
        moe_w1=moe_w1[0], moe_w3=moe_w3[0], moe_w2=moe_w2[0],
        rwkv=dict(shift_mu=rwkv_shift_mu[0], w0=rwkv_w0[0], w_up=rwkv_w_up[0], a0=rwkv_a0[0], a_up=rwkv_a_up[0],
                  g_up=rwkv_g_up[0], k_k=rwkv_k_k[0], k_a=rwkv_k_a[0], r_k=rwkv_r_k[0], ln_g=rwkv_ln_g[0],
                  ln_b=rwkv_ln_b[0]),
        gdn=dict(conv_w=gdn_conv_w[0], A_log=gdn_A_log[0], dt_bias=gdn_dt_bias[0], norm_g=gdn_norm_g[0]),
        mlstm=dict(conv_w=mlstm_conv_w[0], gate_b=mlstm_gate_b[0], norm_g=mlstm_norm_g[0]),
    )
    return (encoder_trunk(x_prompt, p), encoder_trunk(x_sample, p))
```

```python
import functools

import jax
import jax.numpy as jnp
from jax import lax
from jax.experimental import pallas as pl
from jax.experimental.pallas import tpu as pltpu

F32 = jnp.float32
BF16 = jnp.bfloat16

D_MODEL = 1024
HEAD_DIM = 64
A_HEADS = 8
A_W = 512
W_RANK = 64
ICL_RANK = 64
G_RANK = 128
DECAY_SCALE = 0.606531
GN_EPS = 64e-5
A_IN = 1792
B_HEADS = 8
B_W = 512
C_HEADS = 8
C_HEAD_DIM = 128
C_W = 1024
CONV_W = 5
CHUNK = 64
D_FF = 2816
N_EXPERTS = 8
D_FF_EXPERT = 1408
HALO = 8
LANES = 128
PAIR = 2 * HEAD_DIM
N_PAIRS = A_W // PAIR
CHUNK_GROUP = 2
B_IN_PAD = 4 * B_W + LANES
C_IN_PAD = 4 * C_W + 2 * LANES

VMEM_LIMIT = 56 * 1024 * 1024

NT_DIMS = (((1,), (1,)), ((), ()))
TN_DIMS = (((0,), (0,)), ((), ()))


def _dot(a, b):
    return jnp.dot(a.astype(BF16), b.astype(BF16), preferred_element_type=F32)


def _dot_nt(a, b):
    return lax.dot_general(a.astype(BF16), b.astype(BF16), NT_DIMS, preferred_element_type=F32)


def _dot_tn(a, b):
    return lax.dot_general(a.astype(BF16), b.astype(BF16), TN_DIMS, preferred_element_type=F32)


def _split3(x):
    h = x.astype(BF16)
    r = x - h.astype(F32)
    m = r.astype(BF16)
    l = (r - m.astype(F32)).astype(BF16)
    return h, m, l


def _dot_exact_rhs(x, c):
    h, m, l = _split3(x)
    return (jnp.dot(h, c, preferred_element_type=F32) + jnp.dot(m, c, preferred_element_type=F32)
            + jnp.dot(l, c, preferred_element_type=F32))


def _dot_rhs01(x, c):
    return jnp.dot(x.astype(BF16), c, preferred_element_type=F32)


def _dot_exact_lhs(c, x):
    h, m, l = _split3(x)
    return (jnp.dot(c, h, preferred_element_type=F32) + jnp.dot(c, m, preferred_element_type=F32)
            + jnp.dot(c, l, preferred_element_type=F32))


def _dot_tn_exact_rhs(x, c):
    h, m, l = _split3(x)
    f = lambda p: lax.dot_general(p, c, TN_DIMS, preferred_element_type=F32)
    return f(h) + f(m) + f(l)


def _dot3(a, b):
    ah = a.astype(BF16)
    al = (a - ah.astype(F32)).astype(BF16)
    bh = b.astype(BF16)
    bl = (b - bh.astype(F32)).astype(BF16)
    f = lambda p, q: jnp.dot(p, q, preferred_element_type=F32)
    return f(ah, bh) + f(ah, bl) + f(al, bh)


def _sigmoid(x):
    return 1.0 / (1.0 + jnp.exp(-x))


def _silu(x):
    return x * _sigmoid(x)


def _softplus(x):
    return jnp.maximum(x, 0.0) + jnp.log(1.0 + jnp.exp(-jnp.abs(x)))


def _norm_matmul_kernel(x_ref, g_ref, w_ref, *out_refs, widths, tn):
    x = x_ref[...]
    h = (x * lax.rsqrt(jnp.mean(x * x, axis=-1, keepdims=True) + 1e-6) * g_ref[...]).astype(BF16)
    off = 0
    for o_ref, wd in zip(out_refs, widths):
        for c in range(0, wd, tn):
            cw = min(tn, wd - c)
            o_ref[:, c:c + cw] = jnp.dot(h, w_ref[:, off + c:off + c + cw], preferred_element_type=F32)
        off += wd


def norm_matmul(x2d, g, w, widths, tm=256, tn=512):
    n, d = x2d.shape
    ntot = w.shape[1]
    assert sum(widths) == ntot and n % tm == 0
    return pl.pallas_call(
        functools.partial(_norm_matmul_kernel, widths=widths, tn=tn),
        out_shape=[jax.ShapeDtypeStruct((n, wd), F32) for wd in widths],
        grid=(n // tm,),
        in_specs=[pl.BlockSpec((tm, d), lambda i: (i, 0)),
                  pl.BlockSpec((1, d), lambda i: (0, 0)),
                  pl.BlockSpec((d, ntot), lambda i: (0, 0))],
        out_specs=[pl.BlockSpec((tm, wd), lambda i: (i, 0)) for wd in widths],
        compiler_params=pltpu.CompilerParams(dimension_semantics=("parallel",), vmem_limit_bytes=VMEM_LIMIT),
        name="norm_matmul",
    )(x2d, g.reshape(1, d), w)


def _matmul_res_kernel(*refs, n_in):
    y_refs = refs[:n_in]
    w_refs = refs[n_in:2 * n_in]
    x_ref, o_ref = refs[2 * n_in], refs[2 * n_in + 1]
    acc = x_ref[...]
    for y_ref, w_ref in zip(y_refs, w_refs):
        acc = acc + jnp.dot(y_ref[...], w_ref[...], preferred_element_type=F32)
    o_ref[...] = acc


def matmul_residual(ys, ws, x2d, tm=512):
    n, d = x2d.shape
    assert n % tm == 0
    return pl.pallas_call(
        functools.partial(_matmul_res_kernel, n_in=len(ys)),
        out_shape=jax.ShapeDtypeStruct((n, d), F32),
        grid=(n // tm,),
        in_specs=([pl.BlockSpec((tm, y.shape[1]), lambda i: (i, 0)) for y in ys]
                  + [pl.BlockSpec(w.shape, lambda i: (0, 0)) for w in ws]
                  + [pl.BlockSpec((tm, d), lambda i: (i, 0))]),
        out_specs=pl.BlockSpec((tm, d), lambda i: (i, 0)),
        compiler_params=pltpu.CompilerParams(dimension_semantics=("parallel",), vmem_limit_bytes=VMEM_LIMIT),
        name="matmul_residual",
    )(*ys, *ws, x2d)


def _ff_chunks(width, tf):
    return [(c, min(tf, width - c)) for c in range(0, width, tf)]


def _ffn_kernel(x_ref, g_ref, w1_ref, w3_ref, w2_ref, o_ref, *, tf):
    x = x_ref[...]
    h = (x * lax.rsqrt(jnp.mean(x * x, axis=-1, keepdims=True) + 1e-6) * g_ref[...]).astype(BF16)
    acc = x
    for c, cw in _ff_chunks(D_FF, tf):
        a = jnp.dot(h, w1_ref[:, c:c + cw], preferred_element_type=F32)
        b = jnp.dot(h, w3_ref[:, c:c + cw], preferred_element_type=F32)
        acc = acc + jnp.dot((_silu(a) * b).astype(BF16), w2_ref[c:c + cw, :], preferred_element_type=F32)
    o_ref[...] = acc


def ffn_residual(x2d, g, w1, w3, w2, tm=512, tf=256):
    n, d = x2d.shape
    assert n % tm == 0
    const = lambda shape: pl.BlockSpec(shape, lambda i: (0, 0), pipeline_mode=pl.Buffered(1))
    return pl.pallas_call(
        functools.partial(_ffn_kernel, tf=tf),
        out_shape=jax.ShapeDtypeStruct((n, d), F32),
        grid=(n // tm,),
        in_specs=[pl.BlockSpec((tm, d), lambda i: (i, 0)), const((1, d)),
                  const(w1.shape), const(w3.shape), const(w2.shape)],
        out_specs=pl.BlockSpec((tm, d), lambda i: (i, 0)),
        compiler_params=pltpu.CompilerParams(dimension_semantics=("parallel",), vmem_limit_bytes=VMEM_LIMIT),
        name="ffn_residual",
    )(x2d, g.reshape(1, d), w1, w3, w2)


MOE_TM = 256
MOE_TG = 512
ROW_ALIGN = 16


def _router_kernel(x_ref, g_ref, rw_ref, rb_ref, h_ref, gate_ref, sel_ref, cnt_ref):
    x = x_ref[...]
    hf = x * lax.rsqrt(jnp.mean(x * x, axis=-1, keepdims=True) + 1e-6) * g_ref[...]
    h_ref[...] = hf.astype(BF16)
    logits = _dot3(hf, rw_ref[...]) + rb_ref[...]
    lane = lax.broadcasted_iota(jnp.int32, logits.shape, 1)
    m1 = jnp.max(logits, axis=-1, keepdims=True)
    i1 = jnp.min(jnp.where(logits == m1, lane, LANES), axis=-1, keepdims=True)
    rest = jnp.where(lane == i1, -jnp.inf, logits)
    m2 = jnp.max(rest, axis=-1, keepdims=True)
    i2 = jnp.min(jnp.where(rest == m2, lane, LANES), axis=-1, keepdims=True)
    e2 = jnp.exp(m2 - m1)
    den = 1.0 + e2
    gate_ref[...] = jnp.where(lane == i1, 1.0 / den, jnp.where(lane == i2, e2 / den, 0.0))
    sel = jnp.where((lane == i1) | (lane == i2), 1.0, 0.0)
    sel_ref[...] = sel.astype(BF16)
    cnt_ref[...] = jnp.sum(sel, axis=0, keepdims=True)


def _dispatch_kernel(base_ref, h_ref, sel_ref, upper_ref, eye_ref, xs_in_ref, xs_ref, cbuf, sem, *, tm, n_tiles):
    del xs_in_ref
    s = pl.program_id(0)
    slot = s % 2
    sel = sel_ref[...]
    rank_t = lax.dot_general(sel, upper_ref[...], TN_DIMS, preferred_element_type=F32)
    sel_t = lax.dot_general(sel, eye_ref[...], TN_DIMS, preferred_element_type=F32)
    row = lax.broadcasted_iota(jnp.int32, (tm, tm), 0).astype(F32)
    h = h_ref[...]
    for e in range(N_EXPERTS):
        pick = jnp.where((rank_t[e:e + 1, :] == row) & (sel_t[e:e + 1, :] > 0.5), 1.0, 0.0).astype(BF16)
        cbuf[slot, e] = jnp.dot(pick, h, preferred_element_type=F32).astype(BF16)

    def copy(sl, e, tile):
        start = pl.multiple_of(base_ref[tile * N_EXPERTS + e], ROW_ALIGN)
        return pltpu.make_async_copy(cbuf.at[sl, e], xs_ref.at[pl.ds(start, tm)], sem.at[sl, e])

    @pl.when(s > 0)
    def _():
        for e in range(N_EXPERTS):
            copy(1 - slot, e, s - 1).wait()

    for e in range(N_EXPERTS):
        copy(slot, e, s).start()

    @pl.when(s == n_tiles - 1)
    def _():
        for e in range(N_EXPERTS):
            copy(slot, e, s).wait()


def _grouped_ffn_kernel(te_ref, tv_ref, x_ref, w1_ref, w3_ref, w2_ref, o_ref, *, tf):
    del te_ref
    valid = tv_ref[pl.program_id(0)] == 1

    @pl.when(jnp.logical_not(valid))
    def _():
        o_ref[...] = jnp.zeros(o_ref.shape, o_ref.dtype)

    @pl.when(valid)
    def _():
        h = x_ref[...]
        y = jnp.zeros(o_ref.shape, F32)
        for c, cw in _ff_chunks(D_FF_EXPERT, tf):
            a = jnp.dot(h, w1_ref[:, c:c + cw], preferred_element_type=F32)
            b = jnp.dot(h, w3_ref[:, c:c + cw], preferred_element_type=F32)
            y = y + jnp.dot((_silu(a) * b).astype(BF16), w2_ref[c:c + cw, :], preferred_element_type=F32)
        o_ref[...] = y.astype(o_ref.dtype)


def _combine_kernel(base_ref, x_ref, sel_ref, gate_ref, lower_ref, gf_ref, ys_ref, o_ref, ybuf, sem, *, tm, n_tiles):
    s = pl.program_id(0)
    slot = s % 2

    def copy(sl, e, tile):
        start = pl.multiple_of(base_ref[tile * N_EXPERTS + e], ROW_ALIGN)
        return pltpu.make_async_copy(ys_ref.at[pl.ds(start, tm)], ybuf.at[sl, e], sem.at[sl, e])

    @pl.when(s == 0)
    def _():
        for e in range(N_EXPERTS):
            copy(slot, e, s).start()

    @pl.when(s + 1 < n_tiles)
    def _():
        for e in range(N_EXPERTS):
            copy(1 - slot, e, s + 1).start()

    sel = sel_ref[...]
    rank = jnp.dot(lower_ref[...], sel, preferred_element_type=F32)
    self32 = sel.astype(F32)
    gate = gate_ref[...]
    col = lax.broadcasted_iota(jnp.int32, (tm, tm), 1).astype(F32)
    for e in range(N_EXPERTS):
        copy(slot, e, s).wait()
    acc = x_ref[...]
    for e in range(N_EXPERTS):
        pick = jnp.where((rank[:, e:e + 1] == col) & (self32[:, e:e + 1] > 0.5), 1.0, 0.0).astype(BF16)
        acc = acc + gate[:, e:e + 1] * jnp.dot(pick, ybuf[slot, e], preferred_element_type=F32)
    o_ref[...] = acc * lax.rsqrt(jnp.mean(acc * acc, axis=-1, keepdims=True) + 1e-6) * gf_ref[...]


def moe_routed_final(x2d, g, router_w, router_b, w1, w3, w2, g_final, tf=256):
    n, d = x2d.shape
    tm, tg = MOE_TM, MOE_TG
    assert n % tm == 0
    n_tiles = n // tm
    rw = jnp.zeros((d, LANES), F32).at[:, :N_EXPERTS].set(router_w)
    rb = jnp.full((1, LANES), -1e30, F32).at[0, :N_EXPERTS].set(router_b)
    tile = lambda width: pl.BlockSpec((tm, width), lambda i: (i, 0))
    const = lambda shape: pl.BlockSpec(shape, lambda i: (0,) * len(shape))
    h, gates, sel, cnt = pl.pallas_call(
        _router_kernel,
        out_shape=[jax.ShapeDtypeStruct((n, d), BF16), jax.ShapeDtypeStruct((n, LANES), F32),
                   jax.ShapeDtypeStruct((n, LANES), BF16), jax.ShapeDtypeStruct((n_tiles, 1, LANES), F32)],
        grid=(n_tiles,),
        in_specs=[tile(d), const((1, d)), const((d, LANES)), const((1, LANES))],
        out_specs=[tile(d), tile(LANES), tile(LANES), pl.BlockSpec((None, 1, LANES), lambda i: (i, 0, 0))],
        compiler_params=pltpu.CompilerParams(dimension_semantics=("parallel",), vmem_limit_bytes=VMEM_LIMIT),
        name="moe_router",
    )(x2d, g.reshape(1, d), rw, rb)

    cnt = cnt[:, 0, :N_EXPERTS].astype(jnp.int32)
    padded = (cnt + ROW_ALIGN - 1) // ROW_ALIGN * ROW_ALIGN
    total = jnp.sum(padded, axis=0)
    written = total + tm
    size = (written + tg - 1) // tg * tg
    seg_end = jnp.cumsum(size)
    offs = seg_end - size
    base = (offs[None, :] + jnp.cumsum(padded, axis=0) - padded).reshape(-1).astype(jnp.int32)
    p_rows = 2 * n + n_tiles * N_EXPERTS * ROW_ALIGN + N_EXPERTS * (tm + tg)
    p_rows = (p_rows + tg - 1) // tg * tg
    n_gt = p_rows // tg
    tile_start = jnp.arange(n_gt, dtype=jnp.int32) * tg
    tile_expert = jnp.minimum(jnp.sum(tile_start[:, None] >= seg_end[None, :], axis=1), N_EXPERTS - 1).astype(jnp.int32)
    tile_valid = (tile_start < (offs + written)[tile_expert]).astype(jnp.int32)

    r = jnp.arange(tm)
    upper = (r[:, None] < r[None, :]).astype(BF16)
    eye = (r[:, None] == r[None, :]).astype(BF16)
    xs = pl.pallas_call(
        functools.partial(_dispatch_kernel, tm=tm, n_tiles=n_tiles),
        out_shape=jax.ShapeDtypeStruct((p_rows, d), BF16),
        grid_spec=pltpu.PrefetchScalarGridSpec(
            num_scalar_prefetch=1, grid=(n_tiles,),
            in_specs=[pl.BlockSpec((tm, d), lambda i, b: (i, 0)), pl.BlockSpec((tm, LANES), lambda i, b: (i, 0)),
                      pl.BlockSpec((tm, tm), lambda i, b: (0, 0)), pl.BlockSpec((tm, tm), lambda i, b: (0, 0)),
                      pl.BlockSpec(memory_space=pl.ANY)],
            out_specs=pl.BlockSpec(memory_space=pl.ANY),
            scratch_shapes=[pltpu.VMEM((2, N_EXPERTS, tm, d), BF16), pltpu.SemaphoreType.DMA((2, N_EXPERTS))]),
        input_output_aliases={5: 0},
        compiler_params=pltpu.CompilerParams(dimension_semantics=("arbitrary",), vmem_limit_bytes=VMEM_LIMIT),
        name="moe_dispatch",
    )(base, h, sel, upper, eye, jnp.zeros((p_rows, d), BF16))

    ys = pl.pallas_call(
        functools.partial(_grouped_ffn_kernel, tf=tf),
        out_shape=jax.ShapeDtypeStruct((p_rows, d), BF16),
        grid_spec=pltpu.PrefetchScalarGridSpec(
            num_scalar_prefetch=2, grid=(n_gt,),
            in_specs=[pl.BlockSpec((tg, d), lambda j, te, tv: (j, 0)),
                      pl.BlockSpec((None, d, D_FF_EXPERT), lambda j, te, tv: (te[j], 0, 0)),
                      pl.BlockSpec((None, d, D_FF_EXPERT), lambda j, te, tv: (te[j], 0, 0)),
                      pl.BlockSpec((None, D_FF_EXPERT, d), lambda j, te, tv: (te[j], 0, 0))],
            out_specs=pl.BlockSpec((tg, d), lambda j, te, tv: (j, 0))),
        compiler_params=pltpu.CompilerParams(dimension_semantics=("arbitrary",), vmem_limit_bytes=VMEM_LIMIT),
        name="moe_grouped_ffn",
    )(tile_expert, tile_valid, xs, w1, w3, w2)

    lower = (r[:, None] > r[None, :]).astype(BF16)
    return pl.pallas_call(
        functools.partial(_combine_kernel, tm=tm, n_tiles=n_tiles),
        out_shape=jax.ShapeDtypeStruct((n, d), F32),
        grid_spec=pltpu.PrefetchScalarGridSpec(
            num_scalar_prefetch=1, grid=(n_tiles,),
            in_specs=[pl.BlockSpec((tm, d), lambda i, b: (i, 0)), pl.BlockSpec((tm, LANES), lambda i, b: (i, 0)),
                      pl.BlockSpec((tm, LANES), lambda i, b: (i, 0)), pl.BlockSpec((tm, tm), lambda i, b: (0, 0)),
                      pl.BlockSpec((1, d), lambda i, b: (0, 0)), pl.BlockSpec(memory_space=pl.ANY)],
            out_specs=pl.BlockSpec((tm, d), lambda i, b: (i, 0)),
            scratch_shapes=[pltpu.VMEM((2, N_EXPERTS, tm, d), BF16), pltpu.SemaphoreType.DMA((2, N_EXPERTS))]),
        compiler_params=pltpu.CompilerParams(dimension_semantics=("arbitrary",), vmem_limit_bytes=VMEM_LIMIT),
        name="moe_combine",
    )(base, x2d, sel, gates, lower, g_final.reshape(1, d), ys)


def _seq_specs(tb, nt, width, reverse, col_block=0):
    hb = tb // HALO
    nhb = nt * hb
    pos = (lambda t: nt - 1 - t) if reverse else (lambda t: t)
    main = pl.BlockSpec((None, tb, width), lambda b, t: (b, pos(t), col_block))
    prev = pl.BlockSpec((None, HALO, width), lambda b, t: (b, jnp.maximum(pos(t) * hb - 1, 0), col_block))
    nxt = pl.BlockSpec((None, HALO, width), lambda b, t: (b, jnp.minimum((pos(t) + 1) * hb, nhb - 1), col_block))
    return main, prev, nxt


def _fill_extended(xe_ref, x_ref, xp_ref, xn_ref, pos, nt, tb, width=None):
    sl = slice(None) if width is None else slice(0, width)
    xe_ref[0:HALO, :] = jnp.where(pos == 0, 0.0, xp_ref[:, sl])
    xe_ref[HALO:HALO + tb, :] = x_ref[:, sl]
    xe_ref[HALO + tb:2 * HALO + tb, :] = jnp.where(pos == nt - 1, 0.0, xn_ref[:, sl])


def _seg_mats():
    r = lax.broadcasted_iota(jnp.int32, (A_W, A_W), 0) // HEAD_DIM
    c = lax.broadcasted_iota(jnp.int32, (A_W, A_W), 1) // HEAD_DIM
    return (r == c).astype(BF16)


def _tri_const(reverse, reps=1):
    r = jnp.arange(CHUNK)[:, None]
    c = jnp.arange(CHUNK)[None, :]
    m = ((c >= r) if reverse else (c <= r)).astype(BF16)
    return jnp.tile(m, (1, reps))


def _pair_masks(reverse):
    row = lax.broadcasted_iota(jnp.int32, (CHUNK, PAIR), 0)
    lane = lax.broadcasted_iota(jnp.int32, (CHUNK, PAIR), 1)
    col = lane % HEAD_DIM
    strict = (col > row) if reverse else (col < row)
    incl = (col >= row) if reverse else (col <= row)
    eye = (col == row).astype(F32)
    r2 = lax.broadcasted_iota(jnp.int32, (PAIR, PAIR), 0) // HEAD_DIM
    c2 = lax.broadcasted_iota(jnp.int32, (PAIR, PAIR), 1) // HEAD_DIM
    same_head = r2 == c2
    return strict, incl, eye, same_head.astype(BF16), lane < HEAD_DIM, same_head


def _bd(x2, bdm):
    xb = x2.astype(BF16)
    return jnp.concatenate([xb, xb], axis=0) * bdm


def _mm(a, b):
    return jnp.dot(a.astype(BF16), b, preferred_element_type=F32)


def _tri_inverse_levels(n2s, eye2, bdm, out):
    ps = [_mm(n, _bd(n, bdm)) for n in n2s]
    ts = [eye2 + n for n in n2s]
    yield
    for i in range(5):
        if i < 4:
            outs = [_mm(p, jnp.concatenate([_bd(t, bdm), _bd(p, bdm)], axis=1)) for t, p in zip(ts, ps)]
            ps = [o[:, PAIR:] for o in outs]
        else:
            outs = [_mm(p, _bd(t, bdm)) for t, p in zip(ts, ps)]
        ts = [t + o[:, 0:PAIR] for t, o in zip(ts, outs)]
        yield
    out.extend(ts)


def _run_interleaved(*gens):
    live = [g for g in gens if g is not None]
    while live:
        for g in list(live):
            try:
                next(g)
            except StopIteration:
                live.remove(g)


def _software_pipeline(n_groups, phase_a, chain):
    _run_interleaved(phase_a(0))
    for g in range(n_groups):
        _run_interleaved(chain(g), phase_a(g + 1) if g + 1 < n_groups else None)


def _rwkv_kernel(*refs, reverse, final, tb, nt, group):
    if final:
        (x_ref, xp_ref, xn_ref, mu_ref, kk_ref, ka_ref, w0_ref, wup_ref, a0_ref, aup_ref, seg_ref, tri_ref,
         gup_ref, rk_ref, lng_ref, lnb_ref, yf_ref, o_ref,
         xe_ref, s_ref, r_s, k_s, v_s, a_s, b_s, lw_s, y_s) = refs
    else:
        (x_ref, xp_ref, xn_ref, mu_ref, kk_ref, ka_ref, w0_ref, wup_ref, a0_ref, aup_ref, seg_ref, tri_ref,
         o_ref, xe_ref, s_ref, r_s, k_s, v_s, a_s, b_s, lw_s, y_s) = refs
    d = 1 if reverse else 0
    t = pl.program_id(1)
    pos = (nt - 1 - t) if reverse else t
    nc = tb // CHUNK

    @pl.when(t == 0)
    def _():
        s_ref[...] = jnp.zeros(s_ref.shape, F32)

    _fill_extended(xe_ref, x_ref, xp_ref, xn_ref, pos, nt, tb)
    x = x_ref[...]
    nb = 0.5 * (xe_ref[pl.ds(HALO - 1, tb), :] + xe_ref[pl.ds(HALO + 1, tb), :])
    pa = x + mu_ref[...] * (nb - x)
    r = pa[:, 0:A_W]
    k = pa[:, A_W:2 * A_W]
    v = pa[:, 2 * A_W:3 * A_W]
    wd = jnp.tanh(pa[:, 3 * A_W:3 * A_W + W_RANK])
    ad = pa[:, 3 * A_W + W_RANK:3 * A_W + W_RANK + ICL_RANK]
    seg = seg_ref[...]
    kq = k * kk_ref[...]
    kk = kq * lax.rsqrt(_dot_rhs01(kq * kq, seg) + 1e-6)
    lw = -DECAY_SCALE * _sigmoid(w0_ref[d:d + 1, :] + _dot(wd, wup_ref[d]))
    icl = _sigmoid(a0_ref[d:d + 1, :] + _dot(ad, aup_ref[d]))
    ka = ka_ref[...]
    r_s[...] = r
    k_s[...] = k * (1.0 + (icl - 1.0) * ka)
    v_s[...] = v
    a_s[...] = -kk
    b_s[...] = kk * icl
    lw_s[...] = lw

    strict, incl, eye2, bdm, _, same_head = _pair_masks(reverse)
    tri = tri_ref[...]

    groups = [list(range(nc))[j:j + group] for j in range(0, nc, group)]
    if reverse:
        groups = [[nc - 1 - c for c in g] for g in groups]
    staged = {}

    def phase_a(g):
        probs = []
        for c in groups[g]:
            rows = slice(c * CHUNK, (c + 1) * CHUNK)
            lwc = lw_s[rows, :]
            cum = _dot_exact_lhs(tri, lwc)
            last = cum[0:1, :] if reverse else cum[CHUNK - 1:CHUNK, :]
            rt = r_s[rows, :] * jnp.exp(cum)
            at = a_s[rows, :] * jnp.exp(cum - lwc)
            einv = jnp.exp(-cum)
            bc = b_s[rows, :]
            kc = k_s[rows, :]
            bt = bc * einv
            kt = kc * einv
            eend = jnp.exp(last - cum)
            bh = bc * eend
            kh = kc * eend
            gl = jnp.exp(last)
            vc = v_s[rows, :]
            for p in range(N_PAIRS):
                sl = slice(p * PAIR, (p + 1) * PAIR)
                probs.append(dict(c=c, p=p, at=at[:, sl], rt=rt[:, sl], bt=bt[:, sl], kt=kt[:, sl],
                                  bk=jnp.concatenate([bh[:, sl], kh[:, sl]], axis=0).astype(BF16),
                                  gl=gl[:, sl], v=vc[:, sl]))
        yield
        for q in probs:
            xm = jnp.concatenate([q['at'], q['rt']], axis=0).astype(BF16)
            wm = jnp.concatenate([_bd(q.pop('bt'), bdm), _bd(q.pop('kt'), bdm)], axis=0)
            pm = lax.dot_general(xm, wm, NT_DIMS, preferred_element_type=F32)
            q['a_ab'] = jnp.where(strict, pm[0:CHUNK, 0:PAIR], 0.0)
            q['a_kr'] = jnp.concatenate([jnp.where(strict, pm[0:CHUNK, PAIR:], 0.0),
                                         jnp.where(incl, pm[CHUNK:, PAIR:], 0.0)], axis=0).astype(BF16)
            q['a_rb'] = jnp.where(incl, pm[CHUNK:, 0:PAIR], 0.0).astype(BF16)
        yield
        tinvs = []
        yield from _tri_inverse_levels([q.pop('a_ab') for q in probs], eye2, bdm, tinvs)
        for q in probs:
            q['wy'] = _mm(q.pop('a_kr'), _bd(q['v'], bdm))
        yield
        for q, tinv in zip(probs, tinvs):
            wy = q.pop('wy')
            ua = _mm(tinv, jnp.concatenate([_bd(wy[0:CHUNK], bdm), _bd(q.pop('at'), bdm)], axis=1))
            q['u0'] = ua[:, 0:PAIR]
            q['ar'] = jnp.concatenate([ua[:, PAIR:], q.pop('rt')], axis=0).astype(BF16)
            q['y0'] = wy[CHUNK:]
        yield
        staged[g] = probs

    def chain(g):
        probs = staged.pop(g)
        for j in range(0, len(probs), N_PAIRS):
            cp = probs[j:j + N_PAIRS]
            ss = [s_ref[q['p']] for q in cp]
            uys = [lax.dot_general(q['ar'], s.astype(BF16), NT_DIMS, preferred_element_type=F32)
                   for q, s in zip(cp, ss)]
            yield
            ys = []
            for q, s, uy in zip(cp, ss, uys):
                u = uy[0:CHUNK] + q['u0']
                ys.append(uy[CHUNK:] + _mm(q['a_rb'], _bd(u, bdm)) + q['y0'])
                upd = lax.dot_general(jnp.concatenate([u, q['v']], axis=0).astype(BF16), q['bk'], TN_DIMS,
                                      preferred_element_type=F32)
                s_ref[q['p']] = s * q['gl'] + jnp.where(same_head, upd, 0.0)
            c = cp[0]['c']
            y_s[c * CHUNK:(c + 1) * CHUNK, :] = jnp.concatenate(ys, axis=1)
            yield

    _software_pipeline(len(groups), phase_a, chain)

    if not final:
        o_ref[...] = y_s[...]
    else:
        y = yf_ref[...] + y_s[...]
        seg_mean = lambda z: _dot_rhs01(z, seg) * (1.0 / HEAD_DIM)
        mu = seg_mean(y)
        yc = y - mu
        var = seg_mean(yc * yc)
        yn = yc * lax.rsqrt(var + GN_EPS) * lng_ref[...] + lnb_ref[...]
        o = 1 - d
        icl_o = _sigmoid(a0_ref[o:o + 1, :] + _dot(ad, aup_ref[o]))
        ksum = k * (2.0 + (icl + icl_o - 2.0) * ka)
        bonus = _dot_rhs01(r * ksum * rk_ref[...], seg) * v
        gate = _dot(_sigmoid(pa[:, 3 * A_W + W_RANK + ICL_RANK:]), gup_ref[...])
        o_ref[...] = ((yn + bonus) * gate).astype(o_ref.dtype)


def rwkv_pass(pa, p, y_fwd, *, reverse, tb):
    bsz, t_len, _ = pa.shape
    nt = t_len // tb
    final = y_fwd is not None
    main, prev, nxt = _seq_specs(tb, nt, A_IN, reverse)
    const2 = lambda a: pl.BlockSpec(a.shape, lambda b, t: (0,) * a.ndim)
    row = lambda a: a.reshape(1, -1).astype(F32)
    consts = [row(p['shift_mu']), row(p['k_k']), row(p['k_a']), p['w0'].astype(F32), p['w_up'].astype(BF16),
              p['a0'].astype(F32), p['a_up'].astype(BF16), _seg_mats(), _tri_const(reverse)]
    args = [pa, pa, pa] + consts
    in_specs = [main, prev, nxt] + [const2(a) for a in consts]
    pos = (lambda t: nt - 1 - t) if reverse else (lambda t: t)
    io_spec = pl.BlockSpec((None, tb, A_W), lambda b, t: (b, pos(t), 0))
    if final:
        extra = [p['g_up'].astype(BF16), row(p['r_k']), row(p['ln_g']), row(p['ln_b'])]
        args += extra + [y_fwd]
        in_specs += [const2(a) for a in extra] + [io_spec]
    blk = lambda: pltpu.VMEM((tb, A_W), F32)
    return pl.pallas_call(
        functools.partial(_rwkv_kernel, reverse=reverse, final=final, tb=tb, nt=nt, group=CHUNK_GROUP),
        out_shape=jax.ShapeDtypeStruct((bsz, t_len, A_W), BF16 if final else F32),
        grid=(bsz, nt),
        in_specs=in_specs,
        out_specs=io_spec,
        scratch_shapes=[pltpu.VMEM((tb + 2 * HALO, A_IN), F32), pltpu.VMEM((N_PAIRS, PAIR, PAIR), F32),
                        blk(), blk(), blk(), blk(), blk(), blk(), blk()],
        compiler_params=pltpu.CompilerParams(dimension_semantics=("parallel", "arbitrary"),
                                             vmem_limit_bytes=VMEM_LIMIT),
        name="rwkv_bwd_final" if final else "rwkv_fwd",
    )(*args)


def rwkv_mix(pa, p, tb):
    y_f = rwkv_pass(pa, p, None, reverse=False, tb=tb)
    return rwkv_pass(pa, p, y_f, reverse=True, tb=tb)


def _gdn_kernel(*refs, reverse, final, tb, nt, group):
    if final:
        (qkv_ref, z_ref, gl_ref, alog_ref, dtb_ref, seg_ref, tri_ref, trit_ref, exg_ref, exb_ref,
         ng_ref, of_ref, o_ref, s_ref, q_s, k_s, v_s, g_s, bx_s, y_s) = refs
    else:
        (x_ref, xp_ref, xn_ref, cw_ref, alog_ref, dtb_ref, seg_ref, tri_ref, trit_ref, exg_ref, exb_ref,
         o_ref, qkv_out_ref, xe_ref, s_ref, q_s, k_s, v_s, g_s, bx_s, y_s) = refs
    d = 1 if reverse else 0
    t = pl.program_id(1)
    pos = (nt - 1 - t) if reverse else t
    nc = tb // CHUNK
    qkv_w = 3 * B_W

    @pl.when(t == 0)
    def _():
        s_ref[...] = jnp.zeros(s_ref.shape, F32)

    seg = seg_ref[...]
    if final:
        q_s[...] = qkv_ref[:, 0:B_W].astype(F32)
        k_s[...] = qkv_ref[:, B_W:2 * B_W].astype(F32)
        v_s[...] = qkv_ref[:, 2 * B_W:].astype(F32)
        gl = gl_ref[...]
    else:
        _fill_extended(xe_ref, x_ref, xp_ref, xn_ref, pos, nt, tb, width=qkv_w)
        conv = xe_ref[pl.ds(HALO - 2, tb), :] * cw_ref[0:1, :]
        for j in range(1, CONV_W):
            conv = conv + xe_ref[pl.ds(HALO - 2 + j, tb), :] * cw_ref[j:j + 1, :]
        qkv = _silu(conv)
        q = qkv[:, 0:B_W]
        k = qkv[:, B_W:2 * B_W]
        qn = q * lax.rsqrt(_dot_rhs01(q * q, seg) + 1e-6) * (HEAD_DIM ** -0.5)
        kn = k * lax.rsqrt(_dot_rhs01(k * k, seg) + 1e-6)
        q_s[...] = qn
        k_s[...] = kn
        v_s[...] = qkv[:, 2 * B_W:]
        qkv_out_ref[...] = jnp.concatenate([qn, kn, qkv[:, 2 * B_W:]], axis=1).astype(BF16)
        gl = x_ref[:, 4 * B_W:4 * B_W + LANES]
    g_s[...] = -jnp.exp(alog_ref[...]) * _softplus(gl + dtb_ref[...])
    bx_s[...] = _dot_rhs01(_sigmoid(gl), exb_ref[...])

    strict, incl, eye2, bdm, left, same_head = _pair_masks(reverse)
    tri = tri_ref[...]
    trit2 = trit_ref[...]
    exg = exg_ref[...]

    groups = [list(range(nc))[j:j + group] for j in range(0, nc, group)]
    if reverse:
        groups = [[nc - 1 - c for c in g] for g in groups]
    staged = {}

    def phase_a(g):
        probs = []
        for c in groups[g]:
            rows = slice(c * CHUNK, (c + 1) * CHUNK)
            gch = g_s[rows, :]
            gc = _dot_exact_lhs(tri, gch)
            gct = _dot_tn_exact_rhs(gch, trit2)
            last = gc[0:1, :] if reverse else gc[CHUNK - 1:CHUNK, :]
            eg = _dot_rhs01(jnp.exp(gc), exg)
            eend = _dot_rhs01(jnp.exp(last - gc), exg)
            glast = _dot_exact_rhs(jnp.exp(last), exg)
            qc = q_s[rows, :]
            kc = k_s[rows, :]
            vc = v_s[rows, :]
            beta = bx_s[rows, :]
            kb = kc * beta
            vb = vc * beta
            kbe = kb * eg
            qd = qc * eg
            kd = kc * eend
            for p in range(N_PAIRS):
                sl = slice(p * PAIR, (p + 1) * PAIR)
                l0 = d * B_HEADS + 2 * p
                diff = (jnp.where(left, gc[:, l0:l0 + 1], gc[:, l0 + 1:l0 + 2])
                        - jnp.where(left[0:1], gct[l0:l0 + 1, :], gct[l0 + 1:l0 + 2, :]))
                probs.append(dict(c=c, p=p, decay=jnp.exp(jnp.where(incl, diff, 0.0)),
                                  kq=jnp.concatenate([kb[:, sl], qc[:, sl]], axis=0).astype(BF16), k=kc[:, sl],
                                  vb=vb[:, sl], kbe=kbe[:, sl], qd=qd[:, sl], kd=kd[:, sl].astype(BF16),
                                  gl=glast[:, sl]))
        yield
        for q in probs:
            pm = lax.dot_general(q.pop('kq'), _bd(q.pop('k'), bdm), NT_DIMS, preferred_element_type=F32)
            decay = q.pop('decay')
            q['n'] = jnp.where(strict, -pm[0:CHUNK] * decay, 0.0)
            q['qk'] = jnp.where(incl, pm[CHUNK:] * decay, 0.0).astype(BF16)
        yield
        tinvs = []
        yield from _tri_inverse_levels([q.pop('n') for q in probs], eye2, bdm, tinvs)
        for q, tinv in zip(probs, tinvs):
            sol = _mm(tinv, jnp.concatenate([_bd(q.pop('vb'), bdm), _bd(q.pop('kbe'), bdm)], axis=1))
            q['u'] = sol[:, 0:PAIR]
            q['wq'] = jnp.concatenate([sol[:, PAIR:], q.pop('qd')], axis=0).astype(BF16)
        yield
        staged[g] = probs

    def chain(g):
        probs = staged.pop(g)
        for j in range(0, len(probs), N_PAIRS):
            cp = probs[j:j + N_PAIRS]
            ss = [s_ref[q['p']] for q in cp]
            wss = [jnp.dot(q['wq'], s.astype(BF16), preferred_element_type=F32) for q, s in zip(cp, ss)]
            yield
            ys = []
            for q, s, ws in zip(cp, ss, wss):
                v_new = q['u'] - ws[0:CHUNK]
                ys.append(ws[CHUNK:] + _mm(q['qk'], _bd(v_new, bdm)))
                upd = lax.dot_general(q['kd'], v_new.astype(BF16), TN_DIMS, preferred_element_type=F32)
                s_ref[q['p']] = s * q['gl'] + jnp.where(same_head, upd, 0.0)
            c = cp[0]['c']
            y_s[c * CHUNK:(c + 1) * CHUNK, :] = jnp.concatenate(ys, axis=1)
            yield

    _software_pipeline(len(groups), phase_a, chain)

    if not final:
        o_ref[...] = y_s[...]
    else:
        o = of_ref[...] + y_s[...]
        ms = _dot_rhs01(o * o, seg) * (1.0 / HEAD_DIM)
        o_ref[...] = (o * lax.rsqrt(ms + 1e-6) * ng_ref[...] * _silu(z_ref[...])).astype(o_ref.dtype)


def _expand_mat(first_lane):
    r = jnp.arange(LANES)[:, None]
    c = jnp.arange(B_W)[None, :] // HEAD_DIM
    return (r == c + first_lane).astype(BF16)


def gdn_pass(pb, p, fwd_out, *, reverse, tb):
    bsz, t_len, _ = pb.shape
    nt = t_len // tb
    final = fwd_out is not None
    d = 1 if reverse else 0
    const2 = lambda a: pl.BlockSpec(a.shape, lambda b, t: (0,) * a.ndim)
    pad_row = lambda a: jnp.zeros((1, LANES), F32).at[0, :a.size].set(a.reshape(-1))
    consts = [pad_row(p['A_log']), pad_row(p['dt_bias']), _seg_mats(),
              _tri_const(reverse), _tri_const(not reverse, reps=2), _expand_mat(d * B_HEADS),
              _expand_mat(2 * B_HEADS + d * B_HEADS)]
    pos = (lambda t: nt - 1 - t) if reverse else (lambda t: t)
    blk_spec = lambda width, col_block: pl.BlockSpec((None, tb, width), lambda b, t: (b, pos(t), col_block))
    io_spec = blk_spec(B_W, 0)
    blk = lambda: pltpu.VMEM((tb, B_W), F32)
    scratch = [pltpu.VMEM((N_PAIRS, PAIR, PAIR), F32), blk(), blk(), blk(), pltpu.VMEM((tb, LANES), F32),
               blk(), blk()]
    if final:
        o_fwd, qkv = fwd_out
        ng = jnp.tile(p['norm_g'].astype(F32), B_HEADS).reshape(1, B_W)
        args = [qkv, pb, pb] + consts + [ng, o_fwd]
        in_specs = ([blk_spec(3 * B_W, 0), blk_spec(B_W, 3), blk_spec(LANES, 4 * B_W // LANES)]
                    + [const2(a) for a in consts] + [const2(ng), io_spec])
        out_shape = jax.ShapeDtypeStruct((bsz, t_len, B_W), BF16)
        out_specs = io_spec
    else:
        main, prev, nxt = _seq_specs(tb, nt, B_IN_PAD, reverse)
        consts = [p['conv_w'].astype(F32)] + consts
        args = [pb, pb, pb] + consts
        in_specs = [main, prev, nxt] + [const2(a) for a in consts]
        out_shape = [jax.ShapeDtypeStruct((bsz, t_len, B_W), F32), jax.ShapeDtypeStruct((bsz, t_len, 3 * B_W), BF16)]
        out_specs = [io_spec, blk_spec(3 * B_W, 0)]
        scratch = [pltpu.VMEM((tb + 2 * HALO, 3 * B_W), F32)] + scratch
    return pl.pallas_call(
        functools.partial(_gdn_kernel, reverse=reverse, final=final, tb=tb, nt=nt, group=CHUNK_GROUP),
        out_shape=out_shape,
        grid=(bsz, nt),
        in_specs=in_specs,
        out_specs=out_specs,
        scratch_shapes=scratch,
        compiler_params=pltpu.CompilerParams(dimension_semantics=("parallel", "arbitrary"),
                                             vmem_limit_bytes=VMEM_LIMIT),
        name="gdn_bwd_final" if final else "gdn_fwd",
    )(*args)


def gdn_mix(pb, p, tb):
    fwd_out = gdn_pass(pb, p, None, reverse=False, tb=tb)
    return gdn_pass(pb, p, fwd_out, reverse=True, tb=tb)


def _mlstm_kernel(*refs, reverse, final, tb, nt):
    if final:
        (qk_ref, v_ref, og_ref, gt_ref, bi_ref, bf_ref, tri_ref, ng_ref, hf_ref, o_ref,
         c_ref, n_ref, m_ref, q_s, k_s, li_s, lf_s, y_s) = refs
    else:
        (x_ref, xp_ref, xn_ref, cw_ref, bi_ref, bf_ref, tri_ref, o_ref, qk_out_ref,
         xe_ref, c_ref, n_ref, m_ref, q_s, k_s, li_s, lf_s, y_s) = refs
    d = 1 if reverse else 0
    t = pl.program_id(1)
    pos = (nt - 1 - t) if reverse else t
    nc = tb // CHUNK
    qk_w = 2 * C_W
    dh = C_HEAD_DIM

    @pl.when(t == 0)
    def _():
        c_ref[...] = jnp.zeros(c_ref.shape, F32)
        n_ref[...] = jnp.zeros(n_ref.shape, F32)
        m_ref[...] = jnp.zeros(m_ref.shape, F32)

    if final:
        q_s[...] = qk_ref[:, 0:C_W].astype(F32)
        k_s[...] = qk_ref[:, C_W:].astype(F32)
        gates = gt_ref[...]
        v_src, v_col = v_ref, 0
    else:
        _fill_extended(xe_ref, x_ref, xp_ref, xn_ref, pos, nt, tb, width=qk_w)
        conv = xe_ref[pl.ds(HALO - 2, tb), :] * cw_ref[0:1, :]
        for j in range(1, CONV_W):
            conv = conv + xe_ref[pl.ds(HALO - 2 + j, tb), :] * cw_ref[j:j + 1, :]
        qk = _silu(conv)
        qs = qk[:, 0:C_W] * (dh ** -0.5)
        q_s[...] = qs
        k_s[...] = qk[:, C_W:]
        qk_out_ref[...] = jnp.concatenate([qs, qk[:, C_W:]], axis=1).astype(BF16)
        gates = x_ref[:, 4 * C_W:4 * C_W + 2 * LANES]
        v_src, v_col = x_ref, 2 * C_W
    li_s[...] = gates[:, 0:LANES] + bi_ref[...]
    lf_s[...] = -_softplus(-(gates[:, LANES:] + bf_ref[...]))

    row = lax.broadcasted_iota(jnp.int32, (CHUNK, CHUNK), 0)
    col = lax.broadcasted_iota(jnp.int32, (CHUNK, CHUNK), 1)
    incl = (col >= row) if reverse else (col <= row)
    eye_b = (col == row).astype(BF16)
    tri = tri_ref[...]

    row128 = lax.broadcasted_iota(jnp.int32, (CHUNK, LANES), 0)

    def running_max(x):
        s = 1
        while s < CHUNK:
            if reverse:
                shifted = jnp.where(row128 >= CHUNK - s, -jnp.inf, pltpu.roll(x, CHUNK - s, axis=0))
            else:
                shifted = jnp.where(row128 < s, -jnp.inf, pltpu.roll(x, s, axis=0))
            x = jnp.maximum(x, shifted)
            s *= 2
        return x

    state = dict(c=[c_ref[h] for h in range(C_HEADS)], n=[n_ref[h] for h in range(C_HEADS)], m=m_ref[...])

    def chunk_steps(c):
        rows = slice(c * CHUNK, (c + 1) * CHUNK)
        lfc = lf_s[rows, :]
        lic = li_s[rows, :]
        bcum = _dot_exact_lhs(tri, lfc)
        last = bcum[0:1, :] if reverse else bcum[CHUNK - 1:CHUNK, :]
        gkey = lic - bcum
        gkey_t = _dot_tn_exact_rhs(gkey, eye_b)
        m = state['m']
        w_end = last + gkey
        m_new = jnp.maximum(last + m, jnp.max(w_end, axis=0, keepdims=True))
        s_old = jnp.exp(last + m - m_new)
        sk_scale = jnp.exp(w_end - m_new)
        a_inter = bcum + m
        state['m'] = m_new
        m_row = jnp.maximum(a_inter, bcum + running_max(gkey))
        r_row = bcum - m_row
        s_inter = jnp.exp(a_inter - m_row)
        e_mrow = jnp.exp(-m_row)
        qc = q_s[rows, :]
        kc = k_s[rows, :]
        vc = v_src[rows, v_col:v_col + C_W]
        hd = []
        for h in range(C_HEADS):
            sl = slice(h * dh, (h + 1) * dh)
            qh, kh = qc[:, sl].astype(BF16), kc[:, sl]
            hd.append(dict(h=h, l=d * C_HEADS + h, sl=sl, qh=qh, kh=kh, vh=vc[:, sl].astype(BF16),
                           qk=lax.dot_general(qh, kh.astype(BF16), NT_DIMS, preferred_element_type=F32)))
        for e in hd:
            cm, nm = state['c'][e['h']], state['n'][e['h']]
            e['qc'] = jnp.dot(e.pop('qh'), cm.astype(BF16), preferred_element_type=F32)
            e['qn'] = jnp.sum(qc[:, e['sl']] * nm, axis=-1, keepdims=True)
        for e in hd:
            h, l = e['h'], e['l']
            s_k = sk_scale[:, l:l + 1] * e.pop('kh')
            kv = lax.dot_general(s_k.astype(BF16), e['vh'], TN_DIMS, preferred_element_type=F32)
            state['c'][h] = s_old[:, l:l + 1] * state['c'][h] + kv
            state['n'][h] = s_old[:, l:l + 1] * state['n'][h] + jnp.sum(s_k, axis=0, keepdims=True)
        yield
        ys = []
        for e in hd:
            l = e['l']
            expo = jnp.where(incl, r_row[:, l:l + 1] + gkey_t[l:l + 1, :], -jnp.inf)
            pm = e['qk'] * jnp.exp(expo)
            num = s_inter[:, l:l + 1] * e['qc'] + _mm(pm, e['vh'])
            den = s_inter[:, l:l + 1] * e['qn'] + jnp.sum(pm, axis=-1, keepdims=True)
            ys.append(num / jnp.maximum(jnp.abs(den), e_mrow[:, l:l + 1]))
        y_s[rows, :] = jnp.concatenate(ys, axis=1)
        yield

    order = [(nc - 1 - i) if reverse else i for i in range(nc)]
    prev = None
    for c in order:
        cur = chunk_steps(c)
        next(cur)
        if prev is not None:
            next(prev)
        prev = cur
    next(prev)
    for h in range(C_HEADS):
        c_ref[h] = state['c'][h]
        n_ref[h] = state['n'][h]
    m_ref[...] = state['m']

    if not final:
        o_ref[...] = y_s[...]
    else:
        hs = hf_ref[...] + y_s[...]
        outs = []
        for h in range(C_HEADS):
            sl = slice(h * dh, (h + 1) * dh)
            hh = hs[:, sl]
            outs.append(hh * lax.rsqrt(jnp.mean(hh * hh, axis=-1, keepdims=True) + 1e-6))
        hn = jnp.concatenate(outs, axis=1) * ng_ref[...]
        o_ref[...] = (_sigmoid(og_ref[...]) * hn).astype(o_ref.dtype)


def mlstm_pass(pc, p, fwd_out, *, reverse, tb):
    bsz, t_len, _ = pc.shape
    nt = t_len // tb
    final = fwd_out is not None
    const2 = lambda a: pl.BlockSpec(a.shape, lambda b, t: (0,) * a.ndim)
    gate_b = p['gate_b'].astype(F32)
    pad_row = lambda a: jnp.zeros((1, LANES), F32).at[0, :a.size].set(a)
    consts = [pad_row(gate_b[:2 * C_HEADS]), pad_row(gate_b[2 * C_HEADS:]), _tri_const(reverse)]
    pos = (lambda t: nt - 1 - t) if reverse else (lambda t: t)
    blk_spec = lambda width, col_block: pl.BlockSpec((None, tb, width), lambda b, t: (b, pos(t), col_block))
    io_spec = blk_spec(C_W, 0)
    blk = lambda: pltpu.VMEM((tb, C_W), F32)
    gate_blk = lambda: pltpu.VMEM((tb, LANES), F32)
    scratch = [pltpu.VMEM((C_HEADS, C_HEAD_DIM, C_HEAD_DIM), F32), pltpu.VMEM((C_HEADS, 1, C_HEAD_DIM), F32),
               pltpu.VMEM((1, LANES), F32), blk(), blk(), gate_blk(), gate_blk(), blk()]
    if final:
        h_fwd, qk = fwd_out
        ng = p['norm_g'].astype(F32).reshape(1, C_W)
        args = [qk, pc, pc, pc] + consts + [ng, h_fwd]
        in_specs = ([blk_spec(2 * C_W, 0), blk_spec(C_W, 2), blk_spec(C_W, 3), blk_spec(2 * LANES, 2 * C_W // LANES)]
                    + [const2(a) for a in consts] + [const2(ng), io_spec])
        out_shape = jax.ShapeDtypeStruct((bsz, t_len, C_W), BF16)
        out_specs = io_spec
    else:
        main, prev, nxt = _seq_specs(tb, nt, C_IN_PAD, reverse)
        consts = [p['conv_w'].astype(F32)] + consts
        args = [pc, pc, pc] + consts
        in_specs = [main, prev, nxt] + [const2(a) for a in consts]
        out_shape = [jax.ShapeDtypeStruct((bsz, t_len, C_W), F32), jax.ShapeDtypeStruct((bsz, t_len, 2 * C_W), BF16)]
        out_specs = [io_spec, blk_spec(2 * C_W, 0)]
        scratch = [pltpu.VMEM((tb + 2 * HALO, 2 * C_W), F32)] + scratch
    return pl.pallas_call(
        functools.partial(_mlstm_kernel, reverse=reverse, final=final, tb=tb, nt=nt),
        out_shape=out_shape,
        grid=(bsz, nt),
        in_specs=in_specs,
        out_specs=out_specs,
        scratch_shapes=scratch,
        compiler_params=pltpu.CompilerParams(dimension_semantics=("parallel", "arbitrary"),
                                             vmem_limit_bytes=VMEM_LIMIT),
        name="mlstm_bwd_final" if final else "mlstm_fwd",
    )(*args)


def mlstm_mix(pc, p, tb):
    fwd_out = mlstm_pass(pc, p, None, reverse=False, tb=tb)
    return mlstm_pass(pc, p, fwd_out, reverse=True, tb=tb)


def _pad_cols(w, width):
    return jnp.pad(w, ((0, 0), (0, width - w.shape[1])))


def encoder_trunk(x, p, tb=512):
    bsz, t_len, d = x.shape
    n = bsz * t_len
    x2 = x.reshape(n, d)

    w_in = jnp.concatenate([p['even_w_in'][:, :A_IN], _pad_cols(p['even_w_in'][:, A_IN:], B_IN_PAD)], axis=1)
    pa, pb = norm_matmul(x2, p['norm_mix_g'][0], w_in.astype(BF16), (A_IN, B_IN_PAD))
    ya = rwkv_mix(pa.reshape(bsz, t_len, A_IN), p['rwkv'], tb)
    yb = gdn_mix(pb.reshape(bsz, t_len, B_IN_PAD), p['gdn'], tb)
    w_out = p['even_w_out'].astype(BF16)
    x2 = matmul_residual([ya.reshape(n, A_W), yb.reshape(n, B_W)], [w_out[:A_W], w_out[A_W:]], x2)
    x2 = ffn_residual(x2, p['norm_ffn_g'][0], p['ffn_w1'].astype(BF16), p['ffn_w3'].astype(BF16),
                      p['ffn_w2'].astype(BF16))

    w_odd = p['odd_w_in']
    n_gate = 2 * C_HEADS
    w_in = jnp.concatenate([w_odd[:, :4 * C_W], _pad_cols(w_odd[:, 4 * C_W:4 * C_W + n_gate], LANES),
                            _pad_cols(w_odd[:, 4 * C_W + n_gate:], LANES)], axis=1)
    (pc,) = norm_matmul(x2, p['norm_mix_g'][1], w_in.astype(BF16), (C_IN_PAD,))
    yc = mlstm_mix(pc.reshape(bsz, t_len, C_IN_PAD), p['mlstm'], tb)
    x2 = matmul_residual([yc.reshape(n, C_W)], [p['odd_w_out'].astype(BF16)], x2)
    x2 = moe_routed_final(x2, p['norm_ffn_g'][1], p['moe_router_w'], p['moe_router_b'],
                          p['moe_w1'].astype(BF16), p['moe_w3'].astype(BF16), p['moe_w2'].astype(BF16),
                          p['norm_final_g'])
    return x2.reshape(bsz, t_len, d)


def kernel(x_prompt, x_sample, norm_mix_g, norm_ffn_g, norm_final_g, even_w_in, rwkv_shift_mu, rwkv_w0, rwkv_w_up, rwkv_a0, rwkv_a_up, rwkv_g_up, rwkv_k_k, rwkv_k_a, rwkv_r_k, rwkv_ln_g, rwkv_ln_b, gdn_conv_w, gdn_A_log, gdn_dt_bias, gdn_norm_g, even_w_out, odd_w_in, mlstm_conv_w, mlstm_gate_b, mlstm_norm_g, odd_w_out, ffn_w1, ffn_w3, ffn_w2, moe_router_w, moe_router_b, moe_w1, moe_w3, moe_w2):
    p = dict(
        norm_mix_g=norm_mix_g, norm_ffn_g=norm_ffn_g, norm_final_g=norm_final_g,
        even_w_in=even_w_in[0], even_w_out=even_w_out[0], odd_w_in=odd_w_in[0], odd_w_out=odd_w_out[0],
        ffn_w1=ffn_w1[0], ffn_w3=ffn_w3[0], ffn_w2=ffn_w2[0],
        moe_router_w=moe_router_w[0], moe_router_b=moe_router_b[0],
        moe_w1=moe_w1[0], moe_w3=moe_w3[0], moe_w2=moe_w2[0],
        rwkv=dict(shift_mu=rwkv_shift_mu[0], w0=rwkv_w0[0], w_up=rwkv_w_up[0], a0=rwkv_a0[0], a_up=rwkv_a_up[0],
                  g_up=rwkv_g_up[0], k_k=rwkv_k_k[0], k_a=rwkv_k_a[0], r_k=rwkv_r_k[0], ln_g=rwkv_ln_g[0],
                  ln_b=rwkv_ln_b[0]),
        gdn=dict(conv_w=gdn_conv_w[0], A_log=gdn_A_log[0], dt_bias=gdn_dt_bias[0], norm_g=gdn_norm_g[0]),
        mlstm=dict(conv_w=mlstm_conv_w[0], gate_b=mlstm_gate_b[0], norm_g=mlstm_norm_g[0]),
    )
    return (encoder_trunk(x_prompt, p), encoder_trunk(x_sample, p))
```

```python
import functools

import jax
import jax.numpy as jnp
from jax import lax
from jax.experimental import pallas as pl
from jax.experimental.pallas import tpu as pltpu

F32 = jnp.float32
BF16 = jnp.bfloat16

D_MODEL = 1024
HEAD_DIM = 64
A_HEADS = 8
A_W = 512
W_RANK = 64
ICL_RANK = 64
G_RANK = 128
DECAY_SCALE = 0.606531
GN_EPS = 64e-5
A_IN = 1792
B_HEADS = 8
B_W = 512
C_HEADS = 8
C_HEAD_DIM = 128
C_W = 1024
CONV_W = 5
CHUNK = 64
D_FF = 2816
N_EXPERTS = 8
D_FF_EXPERT = 1408
HALO = 8
LANES = 128
PAIR = 2 * HEAD_DIM
N_PAIRS = A_W // PAIR
CHUNK_GROUP = 2
B_IN_PAD = 4 * B_W + LANES
C_IN_PAD = 4 * C_W + 2 * LANES

VMEM_LIMIT = 56 * 1024 * 1024

NT_DIMS = (((1,), (1,)), ((), ()))
TN_DIMS = (((0,), (0,)), ((), ()))


def _dot(a, b):
    return jnp.dot(a.astype(BF16), b.astype(BF16), preferred_element_type=F32)


def _dot_nt(a, b):
    return lax.dot_general(a.astype(BF16), b.astype(BF16), NT_DIMS, preferred_element_type=F32)


def _dot_tn(a, b):
    return lax.dot_general(a.astype(BF16), b.astype(BF16), TN_DIMS, preferred_element_type=F32)


def _split3(x):
    h = x.astype(BF16)
    r = x - h.astype(F32)
    m = r.astype(BF16)
    l = (r - m.astype(F32)).astype(BF16)
    return h, m, l


def _dot_exact_rhs(x, c):
    h, m, l = _split3(x)
    return (jnp.dot(h, c, preferred_element_type=F32) + jnp.dot(m, c, preferred_element_type=F32)
            + jnp.dot(l, c, preferred_element_type=F32))


def _dot_rhs01(x, c):
    return jnp.dot(x.astype(BF16), c, preferred_element_type=F32)


def _dot_exact_lhs(c, x):
    h, m, l = _split3(x)
    return (jnp.dot(c, h, preferred_element_type=F32) + jnp.dot(c, m, preferred_element_type=F32)
            + jnp.dot(c, l, preferred_element_type=F32))


def _dot_tn_exact_rhs(x, c):
    h, m, l = _split3(x)
    f = lambda p: lax.dot_general(p, c, TN_DIMS, preferred_element_type=F32)
    return f(h) + f(m) + f(l)


def _dot3(a, b):
    ah = a.astype(BF16)
    al = (a - ah.astype(F32)).astype(BF16)
    bh = b.astype(BF16)
    bl = (b - bh.astype(F32)).astype(BF16)
    f = lambda p, q: jnp.dot(p, q, preferred_element_type=F32)
    return f(ah, bh) + f(ah, bl) + f(al, bh)


def _sigmoid(x):
    return 1.0 / (1.0 + jnp.exp(-x))


def _silu(x):
    return x * _sigmoid(x)


def _softplus(x):
    return jnp.maximum(x, 0.0) + jnp.log(1.0 + jnp.exp(-jnp.abs(x)))


def _norm_matmul_kernel(x_ref, g_ref, w_ref, *refs, widths, tn, with_rows):
    x = x_ref[...]
    h = (x * lax.rsqrt(jnp.mean(x * x, axis=-1, keepdims=True) + 1e-6) * g_ref[...]).astype(BF16)
    if with_rows:
        wt_ref, refs, ot_ref = refs[0], refs[1:-1], refs[-1]
        ot_ref[...] = lax.dot_general(wt_ref[...], h, NT_DIMS, preferred_element_type=F32)
    off = 0
    for o_ref, wd in zip(refs, widths):
        for c in range(0, wd, tn):
            cw = min(tn, wd - c)
            o_ref[:, c:c + cw] = jnp.dot(h, w_ref[:, off + c:off + c + cw], preferred_element_type=F32)
        off += wd


def norm_matmul(x2d, g, w, widths, w_rows=None, tm=256, tn=512):
    n, d = x2d.shape
    ntot = w.shape[1]
    assert sum(widths) == ntot and n % tm == 0
    args = [x2d, g.reshape(1, d), w]
    in_specs = [pl.BlockSpec((tm, d), lambda i: (i, 0)),
                pl.BlockSpec((1, d), lambda i: (0, 0)),
                pl.BlockSpec((d, ntot), lambda i: (0, 0))]
    out_shape = [jax.ShapeDtypeStruct((n, wd), F32) for wd in widths]
    out_specs = [pl.BlockSpec((tm, wd), lambda i: (i, 0)) for wd in widths]
    if w_rows is not None:
        args.append(w_rows)
        in_specs.append(pl.BlockSpec(w_rows.shape, lambda i: (0, 0)))
        out_shape.append(jax.ShapeDtypeStruct((w_rows.shape[0], n), F32))
        out_specs.append(pl.BlockSpec((w_rows.shape[0], tm), lambda i: (0, i)))
    return pl.pallas_call(
        functools.partial(_norm_matmul_kernel, widths=widths, tn=tn, with_rows=w_rows is not None),
        out_shape=out_shape,
        grid=(n // tm,),
        in_specs=in_specs,
        out_specs=out_specs,
        compiler_params=pltpu.CompilerParams(dimension_semantics=("parallel",), vmem_limit_bytes=VMEM_LIMIT),
        name="norm_matmul",
    )(*args)


def _matmul_res_kernel(*refs, n_in):
    y_refs = refs[:n_in]
    w_refs = refs[n_in:2 * n_in]
    x_ref, o_ref = refs[2 * n_in], refs[2 * n_in + 1]
    acc = x_ref[...]
    for y_ref, w_ref in zip(y_refs, w_refs):
        acc = acc + jnp.dot(y_ref[...], w_ref[...], preferred_element_type=F32)
    o_ref[...] = acc


def matmul_residual(ys, ws, x2d, tm=512):
    n, d = x2d.shape
    assert n % tm == 0
    return pl.pallas_call(
        functools.partial(_matmul_res_kernel, n_in=len(ys)),
        out_shape=jax.ShapeDtypeStruct((n, d), F32),
        grid=(n // tm,),
        in_specs=([pl.BlockSpec((tm, y.shape[1]), lambda i: (i, 0)) for y in ys]
                  + [pl.BlockSpec(w.shape, lambda i: (0, 0)) for w in ws]
                  + [pl.BlockSpec((tm, d), lambda i: (i, 0))]),
        out_specs=pl.BlockSpec((tm, d), lambda i: (i, 0)),
        compiler_params=pltpu.CompilerParams(dimension_semantics=("parallel",), vmem_limit_bytes=VMEM_LIMIT),
        name="matmul_residual",
    )(*ys, *ws, x2d)


def _ff_chunks(width, tf):
    return [(c, min(tf, width - c)) for c in range(0, width, tf)]


def _ffn_kernel(x_ref, g_ref, w1_ref, w3_ref, w2_ref, o_ref, *, tf):
    x = x_ref[...]
    h = (x * lax.rsqrt(jnp.mean(x * x, axis=-1, keepdims=True) + 1e-6) * g_ref[...]).astype(BF16)
    acc = x
    for c, cw in _ff_chunks(D_FF, tf):
        a = jnp.dot(h, w1_ref[:, c:c + cw], preferred_element_type=F32)
        b = jnp.dot(h, w3_ref[:, c:c + cw], preferred_element_type=F32)
        acc = acc + jnp.dot((_silu(a) * b).astype(BF16), w2_ref[c:c + cw, :], preferred_element_type=F32)
    o_ref[...] = acc


def ffn_residual(x2d, g, w1, w3, w2, tm=512, tf=256):
    n, d = x2d.shape
    assert n % tm == 0
    const = lambda shape: pl.BlockSpec(shape, lambda i: (0, 0), pipeline_mode=pl.Buffered(1))
    return pl.pallas_call(
        functools.partial(_ffn_kernel, tf=tf),
        out_shape=jax.ShapeDtypeStruct((n, d), F32),
        grid=(n // tm,),
        in_specs=[pl.BlockSpec((tm, d), lambda i: (i, 0)), const((1, d)),
                  const(w1.shape), const(w3.shape), const(w2.shape)],
        out_specs=pl.BlockSpec((tm, d), lambda i: (i, 0)),
        compiler_params=pltpu.CompilerParams(dimension_semantics=("parallel",), vmem_limit_bytes=VMEM_LIMIT),
        name="ffn_residual",
    )(x2d, g.reshape(1, d), w1, w3, w2)


MOE_TM = 256
MOE_TG = 512
ROW_ALIGN = 16


def _router_kernel(x_ref, g_ref, rw_ref, rb_ref, h_ref, gate_ref, sel_ref, cnt_ref):
    x = x_ref[...]
    hf = x * lax.rsqrt(jnp.mean(x * x, axis=-1, keepdims=True) + 1e-6) * g_ref[...]
    h_ref[...] = hf.astype(BF16)
    logits = _dot3(hf, rw_ref[...]) + rb_ref[...]
    lane = lax.broadcasted_iota(jnp.int32, logits.shape, 1)
    m1 = jnp.max(logits, axis=-1, keepdims=True)
    i1 = jnp.min(jnp.where(logits == m1, lane, LANES), axis=-1, keepdims=True)
    rest = jnp.where(lane == i1, -jnp.inf, logits)
    m2 = jnp.max(rest, axis=-1, keepdims=True)
    i2 = jnp.min(jnp.where(rest == m2, lane, LANES), axis=-1, keepdims=True)
    e2 = jnp.exp(m2 - m1)
    den = 1.0 + e2
    gate_ref[...] = jnp.where(lane == i1, 1.0 / den, jnp.where(lane == i2, e2 / den, 0.0))
    sel = jnp.where((lane == i1) | (lane == i2), 1.0, 0.0)
    sel_ref[...] = sel.astype(BF16)
    cnt_ref[...] = jnp.sum(sel, axis=0, keepdims=True)


def _dispatch_kernel(base_ref, h_ref, sel_ref, upper_ref, eye_ref, xs_in_ref, xs_ref, cbuf, sem, *, tm, n_tiles):
    del xs_in_ref
    s = pl.program_id(0)
    slot = s % 2
    sel = sel_ref[...]
    rank_t = lax.dot_general(sel, upper_ref[...], TN_DIMS, preferred_element_type=F32)
    sel_t = lax.dot_general(sel, eye_ref[...], TN_DIMS, preferred_element_type=F32)
    row = lax.broadcasted_iota(jnp.int32, (tm, tm), 0).astype(F32)
    h = h_ref[...]
    for e in range(N_EXPERTS):
        pick = jnp.where((rank_t[e:e + 1, :] == row) & (sel_t[e:e + 1, :] > 0.5), 1.0, 0.0).astype(BF16)
        cbuf[slot, e] = jnp.dot(pick, h, preferred_element_type=F32).astype(BF16)

    def copy(sl, e, tile):
        start = pl.multiple_of(base_ref[tile * N_EXPERTS + e], ROW_ALIGN)
        return pltpu.make_async_copy(cbuf.at[sl, e], xs_ref.at[pl.ds(start, tm)], sem.at[sl, e])

    @pl.when(s > 0)
    def _():
        for e in range(N_EXPERTS):
            copy(1 - slot, e, s - 1).wait()

    for e in range(N_EXPERTS):
        copy(slot, e, s).start()

    @pl.when(s == n_tiles - 1)
    def _():
        for e in range(N_EXPERTS):
            copy(slot, e, s).wait()


def _grouped_ffn_kernel(te_ref, tv_ref, x_ref, w1_ref, w3_ref, w2_ref, o_ref, *, tf):
    del te_ref
    valid = tv_ref[pl.program_id(0)] == 1

    @pl.when(jnp.logical_not(valid))
    def _():
        o_ref[...] = jnp.zeros(o_ref.shape, o_ref.dtype)

    @pl.when(valid)
    def _():
        h = x_ref[...]
        y = jnp.zeros(o_ref.shape, F32)
        for c, cw in _ff_chunks(D_FF_EXPERT, tf):
            a = jnp.dot(h, w1_ref[:, c:c + cw], preferred_element_type=F32)
            b = jnp.dot(h, w3_ref[:, c:c + cw], preferred_element_type=F32)
            y = y + jnp.dot((_silu(a) * b).astype(BF16), w2_ref[c:c + cw, :], preferred_element_type=F32)
        o_ref[...] = y.astype(o_ref.dtype)


def _combine_kernel(base_ref, x_ref, sel_ref, gate_ref, lower_ref, gf_ref, ys_ref, o_ref, ybuf, sem, *, tm, n_tiles):
    s = pl.program_id(0)
    slot = s % 2

    def copy(sl, e, tile):
        start = pl.multiple_of(base_ref[tile * N_EXPERTS + e], ROW_ALIGN)
        return pltpu.make_async_copy(ys_ref.at[pl.ds(start, tm)], ybuf.at[sl, e], sem.at[sl, e])

    @pl.when(s == 0)
    def _():
        for e in range(N_EXPERTS):
            copy(slot, e, s).start()

    @pl.when(s + 1 < n_tiles)
    def _():
        for e in range(N_EXPERTS):
            copy(1 - slot, e, s + 1).start()

    sel = sel_ref[...]
    rank = jnp.dot(lower_ref[...], sel, preferred_element_type=F32)
    self32 = sel.astype(F32)
    gate = gate_ref[...]
    col = lax.broadcasted_iota(jnp.int32, (tm, tm), 1).astype(F32)
    for e in range(N_EXPERTS):
        copy(slot, e, s).wait()
    acc = x_ref[...]
    for e in range(N_EXPERTS):
        pick = jnp.where((rank[:, e:e + 1] == col) & (self32[:, e:e + 1] > 0.5), 1.0, 0.0).astype(BF16)
        acc = acc + gate[:, e:e + 1] * jnp.dot(pick, ybuf[slot, e], preferred_element_type=F32)
    o_ref[...] = acc * lax.rsqrt(jnp.mean(acc * acc, axis=-1, keepdims=True) + 1e-6) * gf_ref[...]


def moe_routed_final(x2d, g, router_w, router_b, w1, w3, w2, g_final, tf=256):
    n, d = x2d.shape
    tm, tg = MOE_TM, MOE_TG
    assert n % tm == 0
    n_tiles = n // tm
    rw = jnp.zeros((d, LANES), F32).at[:, :N_EXPERTS].set(router_w)
    rb = jnp.full((1, LANES), -1e30, F32).at[0, :N_EXPERTS].set(router_b)
    tile = lambda width: pl.BlockSpec((tm, width), lambda i: (i, 0))
    const = lambda shape: pl.BlockSpec(shape, lambda i: (0,) * len(shape))
    h, gates, sel, cnt = pl.pallas_call(
        _router_kernel,
        out_shape=[jax.ShapeDtypeStruct((n, d), BF16), jax.ShapeDtypeStruct((n, LANES), F32),
                   jax.ShapeDtypeStruct((n, LANES), BF16), jax.ShapeDtypeStruct((n_tiles, 1, LANES), F32)],
        grid=(n_tiles,),
        in_specs=[tile(d), const((1, d)), const((d, LANES)), const((1, LANES))],
        out_specs=[tile(d), tile(LANES), tile(LANES), pl.BlockSpec((None, 1, LANES), lambda i: (i, 0, 0))],
        compiler_params=pltpu.CompilerParams(dimension_semantics=("parallel",), vmem_limit_bytes=VMEM_LIMIT),
        name="moe_router",
    )(x2d, g.reshape(1, d), rw, rb)

    cnt = cnt[:, 0, :N_EXPERTS].astype(jnp.int32)
    padded = (cnt + ROW_ALIGN - 1) // ROW_ALIGN * ROW_ALIGN
    total = jnp.sum(padded, axis=0)
    written = total + tm
    size = (written + tg - 1) // tg * tg
    seg_end = jnp.cumsum(size)
    offs = seg_end - size
    base = (offs[None, :] + jnp.cumsum(padded, axis=0) - padded).reshape(-1).astype(jnp.int32)
    p_rows = 2 * n + n_tiles * N_EXPERTS * ROW_ALIGN + N_EXPERTS * (tm + tg)
    p_rows = (p_rows + tg - 1) // tg * tg
    n_gt = p_rows // tg
    tile_start = jnp.arange(n_gt, dtype=jnp.int32) * tg
    tile_expert = jnp.minimum(jnp.sum(tile_start[:, None] >= seg_end[None, :], axis=1), N_EXPERTS - 1).astype(jnp.int32)
    tile_valid = (tile_start < (offs + written)[tile_expert]).astype(jnp.int32)

    r = jnp.arange(tm)
    upper = (r[:, None] < r[None, :]).astype(BF16)
    eye = (r[:, None] == r[None, :]).astype(BF16)
    xs = pl.pallas_call(
        functools.partial(_dispatch_kernel, tm=tm, n_tiles=n_tiles),
        out_shape=jax.ShapeDtypeStruct((p_rows, d), BF16),
        grid_spec=pltpu.PrefetchScalarGridSpec(
            num_scalar_prefetch=1, grid=(n_tiles,),
            in_specs=[pl.BlockSpec((tm, d), lambda i, b: (i, 0)), pl.BlockSpec((tm, LANES), lambda i, b: (i, 0)),
                      pl.BlockSpec((tm, tm), lambda i, b: (0, 0)), pl.BlockSpec((tm, tm), lambda i, b: (0, 0)),
                      pl.BlockSpec(memory_space=pl.ANY)],
            out_specs=pl.BlockSpec(memory_space=pl.ANY),
            scratch_shapes=[pltpu.VMEM((2, N_EXPERTS, tm, d), BF16), pltpu.SemaphoreType.DMA((2, N_EXPERTS))]),
        input_output_aliases={5: 0},
        compiler_params=pltpu.CompilerParams(dimension_semantics=("arbitrary",), vmem_limit_bytes=VMEM_LIMIT),
        name="moe_dispatch",
    )(base, h, sel, upper, eye, jnp.zeros((p_rows, d), BF16))

    ys = pl.pallas_call(
        functools.partial(_grouped_ffn_kernel, tf=tf),
        out_shape=jax.ShapeDtypeStruct((p_rows, d), BF16),
        grid_spec=pltpu.PrefetchScalarGridSpec(
            num_scalar_prefetch=2, grid=(n_gt,),
            in_specs=[pl.BlockSpec((tg, d), lambda j, te, tv: (j, 0)),
                      pl.BlockSpec((None, d, D_FF_EXPERT), lambda j, te, tv: (te[j], 0, 0)),
                      pl.BlockSpec((None, d, D_FF_EXPERT), lambda j, te, tv: (te[j], 0, 0)),
                      pl.BlockSpec((None, D_FF_EXPERT, d), lambda j, te, tv: (te[j], 0, 0))],
            out_specs=pl.BlockSpec((tg, d), lambda j, te, tv: (j, 0))),
        compiler_params=pltpu.CompilerParams(dimension_semantics=("arbitrary",), vmem_limit_bytes=VMEM_LIMIT),
        name="moe_grouped_ffn",
    )(tile_expert, tile_valid, xs, w1, w3, w2)

    lower = (r[:, None] > r[None, :]).astype(BF16)
    return pl.pallas_call(
        functools.partial(_combine_kernel, tm=tm, n_tiles=n_tiles),
        out_shape=jax.ShapeDtypeStruct((n, d), F32),
        grid_spec=pltpu.PrefetchScalarGridSpec(
            num_scalar_prefetch=1, grid=(n_tiles,),
            in_specs=[pl.BlockSpec((tm, d), lambda i, b: (i, 0)), pl.BlockSpec((tm, LANES), lambda i, b: (i, 0)),
                      pl.BlockSpec((tm, LANES), lambda i, b: (i, 0)), pl.BlockSpec((tm, tm), lambda i, b: (0, 0)),
                      pl.BlockSpec((1, d), lambda i, b: (0, 0)), pl.BlockSpec(memory_space=pl.ANY)],
            out_specs=pl.BlockSpec((tm, d), lambda i, b: (i, 0)),
            scratch_shapes=[pltpu.VMEM((2, N_EXPERTS, tm, d), BF16), pltpu.SemaphoreType.DMA((2, N_EXPERTS))]),
        compiler_params=pltpu.CompilerParams(dimension_semantics=("arbitrary",), vmem_limit_bytes=VMEM_LIMIT),
        name="moe_combine",
    )(base, x2d, sel, gates, lower, g_final.reshape(1, d), ys)


def _seq_specs(tb, nt, width, reverse, col_block=0):
    hb = tb // HALO
    nhb = nt * hb
    pos = (lambda t: nt - 1 - t) if reverse else (lambda t: t)
    main = pl.BlockSpec((None, tb, width), lambda b, t: (b, pos(t), col_block))
    prev = pl.BlockSpec((None, HALO, width), lambda b, t: (b, jnp.maximum(pos(t) * hb - 1, 0), col_block))
    nxt = pl.BlockSpec((None, HALO, width), lambda b, t: (b, jnp.minimum((pos(t) + 1) * hb, nhb - 1), col_block))
    return main, prev, nxt


def _fill_extended(xe_ref, x_ref, xp_ref, xn_ref, pos, nt, tb, width=None):
    sl = slice(None) if width is None else slice(0, width)
    xe_ref[0:HALO, :] = jnp.where(pos == 0, 0.0, xp_ref[:, sl])
    xe_ref[HALO:HALO + tb, :] = x_ref[:, sl]
    xe_ref[HALO + tb:2 * HALO + tb, :] = jnp.where(pos == nt - 1, 0.0, xn_ref[:, sl])


def _seg_mats():
    r = lax.broadcasted_iota(jnp.int32, (A_W, A_W), 0) // HEAD_DIM
    c = lax.broadcasted_iota(jnp.int32, (A_W, A_W), 1) // HEAD_DIM
    return (r == c).astype(BF16)


def _tri_const(reverse, reps=1):
    r = jnp.arange(CHUNK)[:, None]
    c = jnp.arange(CHUNK)[None, :]
    m = ((c >= r) if reverse else (c <= r)).astype(BF16)
    return jnp.tile(m, (1, reps))


def _pair_masks(reverse):
    row = lax.broadcasted_iota(jnp.int32, (CHUNK, PAIR), 0)
    lane = lax.broadcasted_iota(jnp.int32, (CHUNK, PAIR), 1)
    col = lane % HEAD_DIM
    strict = (col > row) if reverse else (col < row)
    incl = (col >= row) if reverse else (col <= row)
    eye = (col == row).astype(F32)
    r2 = lax.broadcasted_iota(jnp.int32, (PAIR, PAIR), 0) // HEAD_DIM
    c2 = lax.broadcasted_iota(jnp.int32, (PAIR, PAIR), 1) // HEAD_DIM
    same_head = r2 == c2
    return strict, incl, eye, same_head.astype(BF16), lane < HEAD_DIM, same_head


def _bd(x2, bdm):
    xb = x2.astype(BF16)
    return jnp.concatenate([xb, xb], axis=0) * bdm


def _mm(a, b):
    return jnp.dot(a.astype(BF16), b, preferred_element_type=F32)


def _tri_inverse_levels(n2s, eye2, bdm, out):
    ps = [_mm(n, _bd(n, bdm)) for n in n2s]
    ts = [eye2 + n for n in n2s]
    yield
    for i in range(5):
        if i < 4:
            outs = [_mm(p, jnp.concatenate([_bd(t, bdm), _bd(p, bdm)], axis=1)) for t, p in zip(ts, ps)]
            ps = [o[:, PAIR:] for o in outs]
        else:
            outs = [_mm(p, _bd(t, bdm)) for t, p in zip(ts, ps)]
        ts = [t + o[:, 0:PAIR] for t, o in zip(ts, outs)]
        yield
    out.extend(ts)


def _run_interleaved(*gens):
    live = [g for g in gens if g is not None]
    while live:
        for g in list(live):
            try:
                next(g)
            except StopIteration:
                live.remove(g)


def _software_pipeline(n_groups, phase_a, chain):
    _run_interleaved(phase_a(0))
    for g in range(n_groups):
        _run_interleaved(chain(g), phase_a(g + 1) if g + 1 < n_groups else None)


def _rwkv_kernel(*refs, reverse, final, tb, nt, group):
    if final:
        (x_ref, xp_ref, xn_ref, mu_ref, kk_ref, ka_ref, w0_ref, wup_ref, a0_ref, aup_ref, seg_ref, tri_ref,
         gup_ref, rk_ref, lng_ref, lnb_ref, yf_ref, o_ref,
         xe_ref, s_ref, r_s, k_s, v_s, a_s, b_s, lw_s, y_s) = refs
    else:
        (x_ref, xp_ref, xn_ref, mu_ref, kk_ref, ka_ref, w0_ref, wup_ref, a0_ref, aup_ref, seg_ref, tri_ref,
         o_ref, xe_ref, s_ref, r_s, k_s, v_s, a_s, b_s, lw_s, y_s) = refs
    d = 1 if reverse else 0
    t = pl.program_id(1)
    pos = (nt - 1 - t) if reverse else t
    nc = tb // CHUNK

    @pl.when(t == 0)
    def _():
        s_ref[...] = jnp.zeros(s_ref.shape, F32)

    _fill_extended(xe_ref, x_ref, xp_ref, xn_ref, pos, nt, tb)
    x = x_ref[...]
    nb = 0.5 * (xe_ref[pl.ds(HALO - 1, tb), :] + xe_ref[pl.ds(HALO + 1, tb), :])
    pa = x + mu_ref[...] * (nb - x)
    r = pa[:, 0:A_W]
    k = pa[:, A_W:2 * A_W]
    v = pa[:, 2 * A_W:3 * A_W]
    wd = jnp.tanh(pa[:, 3 * A_W:3 * A_W + W_RANK])
    ad = pa[:, 3 * A_W + W_RANK:3 * A_W + W_RANK + ICL_RANK]
    seg = seg_ref[...]
    kq = k * kk_ref[...]
    kk = kq * lax.rsqrt(_dot_rhs01(kq * kq, seg) + 1e-6)
    lw = -DECAY_SCALE * _sigmoid(w0_ref[d:d + 1, :] + _dot(wd, wup_ref[d]))
    icl = _sigmoid(a0_ref[d:d + 1, :] + _dot(ad, aup_ref[d]))
    ka = ka_ref[...]
    r_s[...] = r
    k_s[...] = k * (1.0 + (icl - 1.0) * ka)
    v_s[...] = v
    a_s[...] = -kk
    b_s[...] = kk * icl
    lw_s[...] = lw

    strict, incl, eye2, bdm, _, same_head = _pair_masks(reverse)
    tri = tri_ref[...]

    groups = [list(range(nc))[j:j + group] for j in range(0, nc, group)]
    if reverse:
        groups = [[nc - 1 - c for c in g] for g in groups]
    staged = {}

    def phase_a(g):
        probs = []
        for c in groups[g]:
            rows = slice(c * CHUNK, (c + 1) * CHUNK)
            lwc = lw_s[rows, :]
            cum = _dot_exact_lhs(tri, lwc)
            last = cum[0:1, :] if reverse else cum[CHUNK - 1:CHUNK, :]
            rt = r_s[rows, :] * jnp.exp(cum)
            at = a_s[rows, :] * jnp.exp(cum - lwc)
            einv = jnp.exp(-cum)
            bc = b_s[rows, :]
            kc = k_s[rows, :]
            bt = bc * einv
            kt = kc * einv
            eend = jnp.exp(last - cum)
            bh = bc * eend
            kh = kc * eend
            gl = jnp.exp(last)
            vc = v_s[rows, :]
            for p in range(N_PAIRS):
                sl = slice(p * PAIR, (p + 1) * PAIR)
                probs.append(dict(c=c, p=p, at=at[:, sl], rt=rt[:, sl], bt=bt[:, sl], kt=kt[:, sl],
                                  bk=jnp.concatenate([bh[:, sl], kh[:, sl]], axis=0).astype(BF16),
                                  gl=gl[:, sl], v=vc[:, sl]))
        yield
        for q in probs:
            xm = jnp.concatenate([q['at'], q['rt']], axis=0).astype(BF16)
            wm = jnp.concatenate([_bd(q.pop('bt'), bdm), _bd(q.pop('kt'), bdm)], axis=0)
            pm = lax.dot_general(xm, wm, NT_DIMS, preferred_element_type=F32)
            q['a_ab'] = jnp.where(strict, pm[0:CHUNK, 0:PAIR], 0.0)
            q['a_kr'] = jnp.concatenate([jnp.where(strict, pm[0:CHUNK, PAIR:], 0.0),
                                         jnp.where(incl, pm[CHUNK:, PAIR:], 0.0)], axis=0).astype(BF16)
            q['a_rb'] = jnp.where(incl, pm[CHUNK:, 0:PAIR], 0.0).astype(BF16)
        yield
        tinvs = []
        yield from _tri_inverse_levels([q.pop('a_ab') for q in probs], eye2, bdm, tinvs)
        for q in probs:
            q['wy'] = _mm(q.pop('a_kr'), _bd(q['v'], bdm))
        yield
        for q, tinv in zip(probs, tinvs):
            wy = q.pop('wy')
            ua = _mm(tinv, jnp.concatenate([_bd(wy[0:CHUNK], bdm), _bd(q.pop('at'), bdm)], axis=1))
            q['u0'] = ua[:, 0:PAIR]
            q['ar'] = jnp.concatenate([ua[:, PAIR:], q.pop('rt')], axis=0).astype(BF16)
            q['y0'] = wy[CHUNK:]
        yield
        staged[g] = probs

    def chain(g):
        probs = staged.pop(g)
        for j in range(0, len(probs), N_PAIRS):
            cp = probs[j:j + N_PAIRS]
            ss = [s_ref[q['p']] for q in cp]
            uys = [lax.dot_general(q['ar'], s.astype(BF16), NT_DIMS, preferred_element_type=F32)
                   for q, s in zip(cp, ss)]
            yield
            ys = []
            for q, s, uy in zip(cp, ss, uys):
                u = uy[0:CHUNK] + q['u0']
                ys.append(uy[CHUNK:] + _mm(q['a_rb'], _bd(u, bdm)) + q['y0'])
                upd = lax.dot_general(jnp.concatenate([u, q['v']], axis=0).astype(BF16), q['bk'], TN_DIMS,
                                      preferred_element_type=F32)
                s_ref[q['p']] = s * q['gl'] + jnp.where(same_head, upd, 0.0)
            c = cp[0]['c']
            y_s[c * CHUNK:(c + 1) * CHUNK, :] = jnp.concatenate(ys, axis=1)
            yield

    _software_pipeline(len(groups), phase_a, chain)

    if not final:
        o_ref[...] = y_s[...]
    else:
        y = yf_ref[...] + y_s[...]
        seg_mean = lambda z: _dot_rhs01(z, seg) * (1.0 / HEAD_DIM)
        mu = seg_mean(y)
        yc = y - mu
        var = seg_mean(yc * yc)
        yn = yc * lax.rsqrt(var + GN_EPS) * lng_ref[...] + lnb_ref[...]
        o = 1 - d
        icl_o = _sigmoid(a0_ref[o:o + 1, :] + _dot(ad, aup_ref[o]))
        ksum = k * (2.0 + (icl + icl_o - 2.0) * ka)
        bonus = _dot_rhs01(r * ksum * rk_ref[...], seg) * v
        gate = _dot(_sigmoid(pa[:, 3 * A_W + W_RANK + ICL_RANK:]), gup_ref[...])
        o_ref[...] = ((yn + bonus) * gate).astype(o_ref.dtype)


def rwkv_pass(pa, p, y_fwd, *, reverse, tb):
    bsz, t_len, _ = pa.shape
    nt = t_len // tb
    final = y_fwd is not None
    main, prev, nxt = _seq_specs(tb, nt, A_IN, reverse)
    const2 = lambda a: pl.BlockSpec(a.shape, lambda b, t: (0,) * a.ndim)
    row = lambda a: a.reshape(1, -1).astype(F32)
    consts = [row(p['shift_mu']), row(p['k_k']), row(p['k_a']), p['w0'].astype(F32), p['w_up'].astype(BF16),
              p['a0'].astype(F32), p['a_up'].astype(BF16), _seg_mats(), _tri_const(reverse)]
    args = [pa, pa, pa] + consts
    in_specs = [main, prev, nxt] + [const2(a) for a in consts]
    pos = (lambda t: nt - 1 - t) if reverse else (lambda t: t)
    io_spec = pl.BlockSpec((None, tb, A_W), lambda b, t: (b, pos(t), 0))
    if final:
        extra = [p['g_up'].astype(BF16), row(p['r_k']), row(p['ln_g']), row(p['ln_b'])]
        args += extra + [y_fwd]
        in_specs += [const2(a) for a in extra] + [io_spec]
    blk = lambda: pltpu.VMEM((tb, A_W), F32)
    return pl.pallas_call(
        functools.partial(_rwkv_kernel, reverse=reverse, final=final, tb=tb, nt=nt, group=CHUNK_GROUP),
        out_shape=jax.ShapeDtypeStruct((bsz, t_len, A_W), BF16 if final else F32),
        grid=(bsz, nt),
        in_specs=in_specs,
        out_specs=io_spec,
        scratch_shapes=[pltpu.VMEM((tb + 2 * HALO, A_IN), F32), pltpu.VMEM((N_PAIRS, PAIR, PAIR), F32),
                        blk(), blk(), blk(), blk(), blk(), blk(), blk()],
        compiler_params=pltpu.CompilerParams(dimension_semantics=("parallel", "arbitrary"),
                                             vmem_limit_bytes=VMEM_LIMIT),
        name="rwkv_bwd_final" if final else "rwkv_fwd",
    )(*args)


def rwkv_mix(pa, p, tb):
    y_f = rwkv_pass(pa, p, None, reverse=False, tb=tb)
    return rwkv_pass(pa, p, y_f, reverse=True, tb=tb)


def _gdn_kernel(*refs, reverse, final, tb, nt, group):
    if final:
        (qkv_ref, z_ref, gl_ref, alog_ref, dtb_ref, seg_ref, tri_ref, trit_ref, exg_ref, exb_ref,
         ng_ref, of_ref, o_ref, s_ref, q_s, k_s, v_s, g_s, bx_s, y_s) = refs
    else:
        (x_ref, xp_ref, xn_ref, cw_ref, alog_ref, dtb_ref, seg_ref, tri_ref, trit_ref, exg_ref, exb_ref,
         o_ref, qkv_out_ref, xe_ref, s_ref, q_s, k_s, v_s, g_s, bx_s, y_s) = refs
    d = 1 if reverse else 0
    t = pl.program_id(1)
    pos = (nt - 1 - t) if reverse else t
    nc = tb // CHUNK
    qkv_w = 3 * B_W

    @pl.when(t == 0)
    def _():
        s_ref[...] = jnp.zeros(s_ref.shape, F32)

    seg = seg_ref[...]
    if final:
        q_s[...] = qkv_ref[:, 0:B_W].astype(F32)
        k_s[...] = qkv_ref[:, B_W:2 * B_W].astype(F32)
        v_s[...] = qkv_ref[:, 2 * B_W:].astype(F32)
        gl = gl_ref[...]
    else:
        _fill_extended(xe_ref, x_ref, xp_ref, xn_ref, pos, nt, tb, width=qkv_w)
        conv = xe_ref[pl.ds(HALO - 2, tb), :] * cw_ref[0:1, :]
        for j in range(1, CONV_W):
            conv = conv + xe_ref[pl.ds(HALO - 2 + j, tb), :] * cw_ref[j:j + 1, :]
        qkv = _silu(conv)
        q = qkv[:, 0:B_W]
        k = qkv[:, B_W:2 * B_W]
        qn = q * lax.rsqrt(_dot_rhs01(q * q, seg) + 1e-6) * (HEAD_DIM ** -0.5)
        kn = k * lax.rsqrt(_dot_rhs01(k * k, seg) + 1e-6)
        q_s[...] = qn
        k_s[...] = kn
        v_s[...] = qkv[:, 2 * B_W:]
        qkv_out_ref[...] = jnp.concatenate([qn, kn, qkv[:, 2 * B_W:]], axis=1).astype(BF16)
        gl = x_ref[:, 4 * B_W:4 * B_W + LANES]
    g_s[...] = -jnp.exp(alog_ref[...]) * _softplus(gl + dtb_ref[...])
    bx_s[...] = _dot_rhs01(_sigmoid(gl), exb_ref[...])

    strict, incl, eye2, bdm, left, same_head = _pair_masks(reverse)
    tri = tri_ref[...]
    trit2 = trit_ref[...]
    exg = exg_ref[...]

    groups = [list(range(nc))[j:j + group] for j in range(0, nc, group)]
    if reverse:
        groups = [[nc - 1 - c for c in g] for g in groups]
    staged = {}

    def phase_a(g):
        probs = []
        for c in groups[g]:
            rows = slice(c * CHUNK, (c + 1) * CHUNK)
            gch = g_s[rows, :]
            gc = _dot_exact_lhs(tri, gch)
            gct = _dot_tn_exact_rhs(gch, trit2)
            last = gc[0:1, :] if reverse else gc[CHUNK - 1:CHUNK, :]
            eg = _dot_rhs01(jnp.exp(gc), exg)
            eend = _dot_rhs01(jnp.exp(last - gc), exg)
            glast = _dot_exact_rhs(jnp.exp(last), exg)
            qc = q_s[rows, :]
            kc = k_s[rows, :]
            vc = v_s[rows, :]
            beta = bx_s[rows, :]
            kb = kc * beta
            vb = vc * beta
            kbe = kb * eg
            qd = qc * eg
            kd = kc * eend
            for p in range(N_PAIRS):
                sl = slice(p * PAIR, (p + 1) * PAIR)
                l0 = d * B_HEADS + 2 * p
                diff = (jnp.where(left, gc[:, l0:l0 + 1], gc[:, l0 + 1:l0 + 2])
                        - jnp.where(left[0:1], gct[l0:l0 + 1, :], gct[l0 + 1:l0 + 2, :]))
                probs.append(dict(c=c, p=p, decay=jnp.exp(jnp.where(incl, diff, 0.0)),
                                  kq=jnp.concatenate([kb[:, sl], qc[:, sl]], axis=0).astype(BF16), k=kc[:, sl],
                                  vb=vb[:, sl], kbe=kbe[:, sl], qd=qd[:, sl], kd=kd[:, sl].astype(BF16),
                                  gl=glast[:, sl]))
        yield
        for q in probs:
            pm = lax.dot_general(q.pop('kq'), _bd(q.pop('k'), bdm), NT_DIMS, preferred_element_type=F32)
            decay = q.pop('decay')
            q['n'] = jnp.where(strict, -pm[0:CHUNK] * decay, 0.0)
            q['qk'] = jnp.where(incl, pm[CHUNK:] * decay, 0.0).astype(BF16)
        yield
        tinvs = []
        yield from _tri_inverse_levels([q.pop('n') for q in probs], eye2, bdm, tinvs)
        for q, tinv in zip(probs, tinvs):
            sol = _mm(tinv, jnp.concatenate([_bd(q.pop('vb'), bdm), _bd(q.pop('kbe'), bdm)], axis=1))
            q['u'] = sol[:, 0:PAIR]
            q['wq'] = jnp.concatenate([sol[:, PAIR:], q.pop('qd')], axis=0).astype(BF16)
        yield
        staged[g] = probs

    def chain(g):
        probs = staged.pop(g)
        for j in range(0, len(probs), N_PAIRS):
            cp = probs[j:j + N_PAIRS]
            ss = [s_ref[q['p']] for q in cp]
            wss = [jnp.dot(q['wq'], s.astype(BF16), preferred_element_type=F32) for q, s in zip(cp, ss)]
            yield
            ys = []
            for q, s, ws in zip(cp, ss, wss):
                v_new = q['u'] - ws[0:CHUNK]
                ys.append(ws[CHUNK:] + _mm(q['qk'], _bd(v_new, bdm)))
                upd = lax.dot_general(q['kd'], v_new.astype(BF16), TN_DIMS, preferred_element_type=F32)
                s_ref[q['p']] = s * q['gl'] + jnp.where(same_head, upd, 0.0)
            c = cp[0]['c']
            y_s[c * CHUNK:(c + 1) * CHUNK, :] = jnp.concatenate(ys, axis=1)
            yield

    _software_pipeline(len(groups), phase_a, chain)

    if not final:
        o_ref[...] = y_s[...]
    else:
        o = of_ref[...] + y_s[...]
        ms = _dot_rhs01(o * o, seg) * (1.0 / HEAD_DIM)
        o_ref[...] = (o * lax.rsqrt(ms + 1e-6) * ng_ref[...] * _silu(z_ref[...])).astype(o_ref.dtype)


def _expand_mat(first_lane):
    r = jnp.arange(LANES)[:, None]
    c = jnp.arange(B_W)[None, :] // HEAD_DIM
    return (r == c + first_lane).astype(BF16)


def gdn_pass(pb, p, fwd_out, *, reverse, tb):
    bsz, t_len, _ = pb.shape
    nt = t_len // tb
    final = fwd_out is not None
    d = 1 if reverse else 0
    const2 = lambda a: pl.BlockSpec(a.shape, lambda b, t: (0,) * a.ndim)
    pad_row = lambda a: jnp.zeros((1, LANES), F32).at[0, :a.size].set(a.reshape(-1))
    consts = [pad_row(p['A_log']), pad_row(p['dt_bias']), _seg_mats(),
              _tri_const(reverse), _tri_const(not reverse, reps=2), _expand_mat(d * B_HEADS),
              _expand_mat(2 * B_HEADS + d * B_HEADS)]
    pos = (lambda t: nt - 1 - t) if reverse else (lambda t: t)
    blk_spec = lambda width, col_block: pl.BlockSpec((None, tb, width), lambda b, t: (b, pos(t), col_block))
    io_spec = blk_spec(B_W, 0)
    blk = lambda: pltpu.VMEM((tb, B_W), F32)
    scratch = [pltpu.VMEM((N_PAIRS, PAIR, PAIR), F32), blk(), blk(), blk(), pltpu.VMEM((tb, LANES), F32),
               blk(), blk()]
    if final:
        o_fwd, qkv = fwd_out
        ng = jnp.tile(p['norm_g'].astype(F32), B_HEADS).reshape(1, B_W)
        args = [qkv, pb, pb] + consts + [ng, o_fwd]
        in_specs = ([blk_spec(3 * B_W, 0), blk_spec(B_W, 3), blk_spec(LANES, 4 * B_W // LANES)]
                    + [const2(a) for a in consts] + [const2(ng), io_spec])
        out_shape = jax.ShapeDtypeStruct((bsz, t_len, B_W), BF16)
        out_specs = io_spec
    else:
        main, prev, nxt = _seq_specs(tb, nt, B_IN_PAD, reverse)
        consts = [p['conv_w'].astype(F32)] + consts
        args = [pb, pb, pb] + consts
        in_specs = [main, prev, nxt] + [const2(a) for a in consts]
        out_shape = [jax.ShapeDtypeStruct((bsz, t_len, B_W), F32), jax.ShapeDtypeStruct((bsz, t_len, 3 * B_W), BF16)]
        out_specs = [io_spec, blk_spec(3 * B_W, 0)]
        scratch = [pltpu.VMEM((tb + 2 * HALO, 3 * B_W), F32)] + scratch
    return pl.pallas_call(
        functools.partial(_gdn_kernel, reverse=reverse, final=final, tb=tb, nt=nt, group=CHUNK_GROUP),
        out_shape=out_shape,
        grid=(bsz, nt),
        in_specs=in_specs,
        out_specs=out_specs,
        scratch_shapes=scratch,
        compiler_params=pltpu.CompilerParams(dimension_semantics=("parallel", "arbitrary"),
                                             vmem_limit_bytes=VMEM_LIMIT),
        name="gdn_bwd_final" if final else "gdn_fwd",
    )(*args)


def gdn_mix(pb, p, tb):
    fwd_out = gdn_pass(pb, p, None, reverse=False, tb=tb)
    return gdn_pass(pb, p, fwd_out, reverse=True, tb=tb)


def _mlstm_kernel(*refs, reverse, final, tb, nt):
    if final:
        (qk_ref, v_ref, og_ref, gt_ref, bi_ref, bf_ref, tri_ref, ng_ref, hf_ref, o_ref,
         c_ref, m_ref, q_s, k_s, li_s, lf_s, y_s) = refs
    else:
        (x_ref, xp_ref, xn_ref, cw_ref, bi_ref, bf_ref, tri_ref, o_ref, qk_out_ref,
         xe_ref, c_ref, m_ref, q_s, k_s, li_s, lf_s, y_s) = refs
    d = 1 if reverse else 0
    t = pl.program_id(1)
    pos = (nt - 1 - t) if reverse else t
    nc = tb // CHUNK
    qk_w = 2 * C_W
    dh = C_HEAD_DIM

    @pl.when(t == 0)
    def _():
        c_ref[...] = jnp.zeros(c_ref.shape, F32)
        m_ref[...] = jnp.zeros(m_ref.shape, F32)

    if final:
        q_s[...] = qk_ref[:, 0:C_W].astype(F32)
        k_s[...] = qk_ref[:, C_W:].astype(F32)
        gates = gt_ref[...]
        v_src, v_col = v_ref, 0
    else:
        _fill_extended(xe_ref, x_ref, xp_ref, xn_ref, pos, nt, tb, width=qk_w)
        conv = xe_ref[pl.ds(HALO - 2, tb), :] * cw_ref[0:1, :]
        for j in range(1, CONV_W):
            conv = conv + xe_ref[pl.ds(HALO - 2 + j, tb), :] * cw_ref[j:j + 1, :]
        qk = _silu(conv)
        qs = qk[:, 0:C_W] * (dh ** -0.5)
        q_s[...] = qs
        k_s[...] = qk[:, C_W:]
        qk_out_ref[...] = jnp.concatenate([qs, qk[:, C_W:]], axis=1).astype(BF16)
        gates = x_ref[:, 4 * C_W:4 * C_W + 2 * LANES]
        v_src, v_col = x_ref, 2 * C_W
    li_s[...] = gates[:, 0:LANES] + bi_ref[...]
    lf_s[...] = -_softplus(-(gates[:, LANES:] + bf_ref[...]))

    row = lax.broadcasted_iota(jnp.int32, (CHUNK, CHUNK), 0)
    col = lax.broadcasted_iota(jnp.int32, (CHUNK, CHUNK), 1)
    incl = (col >= row) if reverse else (col <= row)
    eye_b = (col == row).astype(BF16)
    tri = tri_ref[...]

    row128 = lax.broadcasted_iota(jnp.int32, (CHUNK, LANES), 0)

    def running_max(x):
        s = 1
        while s < CHUNK:
            if reverse:
                shifted = jnp.where(row128 >= CHUNK - s, -jnp.inf, pltpu.roll(x, CHUNK - s, axis=0))
            else:
                shifted = jnp.where(row128 < s, -jnp.inf, pltpu.roll(x, s, axis=0))
            x = jnp.maximum(x, shifted)
            s *= 2
        return x

    state = dict(c=[c_ref[h] for h in range(C_HEADS)], m=m_ref[...])
    ones_b = jnp.ones((CHUNK, dh), BF16)

    def gate_pre(c):
        rows = slice(c * CHUNK, (c + 1) * CHUNK)
        bcum = _dot_exact_lhs(tri, lf_s[rows, :])
        gkey = li_s[rows, :] - bcum
        return bcum, gkey, _dot_tn_exact_rhs(gkey, eye_b), bcum + running_max(gkey)

    pre = {c: gate_pre(c) for c in range(nc)}

    def chunk_steps(c):
        rows = slice(c * CHUNK, (c + 1) * CHUNK)
        bcum, gkey, gkey_t, intra_max = pre.pop(c)
        last = bcum[0:1, :] if reverse else bcum[CHUNK - 1:CHUNK, :]
        m = state['m']
        w_end = last + gkey
        m_new = jnp.maximum(last + m, jnp.max(w_end, axis=0, keepdims=True))
        s_old = jnp.exp(last + m - m_new)
        sk_scale = jnp.exp(w_end - m_new)
        a_inter = bcum + m
        state['m'] = m_new
        m_row = jnp.maximum(a_inter, intra_max)
        r_row = bcum - m_row
        s_inter = jnp.exp(a_inter - m_row)
        e_mrow = jnp.exp(-m_row)
        qc = q_s[rows, :]
        kc = k_s[rows, :]
        vc = v_src[rows, v_col:v_col + C_W]
        hd = []
        for h in range(C_HEADS):
            sl = slice(h * dh, (h + 1) * dh)
            qh, kh = qc[:, sl].astype(BF16), kc[:, sl]
            hd.append(dict(h=h, l=d * C_HEADS + h, qh=qh, kh=kh,
                           va=jnp.concatenate([vc[:, sl].astype(BF16), ones_b], axis=1),
                           qk=lax.dot_general(qh, kh.astype(BF16), NT_DIMS, preferred_element_type=F32)))
        for e in hd:
            e['qcn'] = jnp.dot(e.pop('qh'), state['c'][e['h']].astype(BF16), preferred_element_type=F32)
        for e in hd:
            h, l = e['h'], e['l']
            s_k = sk_scale[:, l:l + 1] * e.pop('kh')
            upd = lax.dot_general(s_k.astype(BF16), e['va'], TN_DIMS, preferred_element_type=F32)
            state['c'][h] = s_old[:, l:l + 1] * state['c'][h] + upd
        yield
        ys = []
        for e in hd:
            l = e['l']
            expo = jnp.where(incl, r_row[:, l:l + 1] + gkey_t[l:l + 1, :], -jnp.inf)
            pva = _mm(e['qk'] * jnp.exp(expo), e['va'])
            qcn = e['qcn']
            num = s_inter[:, l:l + 1] * qcn[:, 0:dh] + pva[:, 0:dh]
            den = s_inter[:, l:l + 1] * qcn[:, dh:] + pva[:, dh:]
            ys.append(num / jnp.maximum(jnp.abs(den), e_mrow[:, l:l + 1]))
        y_s[rows, :] = jnp.concatenate(ys, axis=1)
        yield

    order = [(nc - 1 - i) if reverse else i for i in range(nc)]
    prev = None
    for c in order:
        cur = chunk_steps(c)
        next(cur)
        if prev is not None:
            next(prev)
        prev = cur
    next(prev)
    for h in range(C_HEADS):
        c_ref[h] = state['c'][h]
    m_ref[...] = state['m']

    if not final:
        o_ref[...] = y_s[...]
    else:
        hs = hf_ref[...] + y_s[...]
        outs = []
        ones_sq = jnp.ones((dh, dh), BF16)
        for h in range(C_HEADS):
            sl = slice(h * dh, (h + 1) * dh)
            hh = hs[:, sl]
            outs.append(hh * lax.rsqrt(_mm(hh * hh, ones_sq) * (1.0 / dh) + 1e-6))
        hn = jnp.concatenate(outs, axis=1) * ng_ref[...]
        o_ref[...] = (_sigmoid(og_ref[...]) * hn).astype(o_ref.dtype)


def mlstm_pass(pc, p, fwd_out, *, reverse, tb):
    bsz, t_len, _ = pc.shape
    nt = t_len // tb
    final = fwd_out is not None
    const2 = lambda a: pl.BlockSpec(a.shape, lambda b, t: (0,) * a.ndim)
    gate_b = p['gate_b'].astype(F32)
    pad_row = lambda a: jnp.zeros((1, LANES), F32).at[0, :a.size].set(a)
    consts = [pad_row(gate_b[:2 * C_HEADS]), pad_row(gate_b[2 * C_HEADS:]), _tri_const(reverse)]
    pos = (lambda t: nt - 1 - t) if reverse else (lambda t: t)
    blk_spec = lambda width, col_block: pl.BlockSpec((None, tb, width), lambda b, t: (b, pos(t), col_block))
    io_spec = blk_spec(C_W, 0)
    blk = lambda: pltpu.VMEM((tb, C_W), F32)
    gate_blk = lambda: pltpu.VMEM((tb, LANES), F32)
    scratch = [pltpu.VMEM((C_HEADS, C_HEAD_DIM, 2 * C_HEAD_DIM), F32),
               pltpu.VMEM((1, LANES), F32), blk(), blk(), gate_blk(), gate_blk(), blk()]
    if final:
        h_fwd, qk = fwd_out
        ng = p['norm_g'].astype(F32).reshape(1, C_W)
        args = [qk, pc, pc, pc] + consts + [ng, h_fwd]
        in_specs = ([blk_spec(2 * C_W, 0), blk_spec(C_W, 2), blk_spec(C_W, 3), blk_spec(2 * LANES, 2 * C_W // LANES)]
                    + [const2(a) for a in consts] + [const2(ng), io_spec])
        out_shape = jax.ShapeDtypeStruct((bsz, t_len, C_W), BF16)
        out_specs = io_spec
    else:
        main, prev, nxt = _seq_specs(tb, nt, C_IN_PAD, reverse)
        consts = [p['conv_w'].astype(F32)] + consts
        args = [pc, pc, pc] + consts
        in_specs = [main, prev, nxt] + [const2(a) for a in consts]
        out_shape = [jax.ShapeDtypeStruct((bsz, t_len, C_W), F32), jax.ShapeDtypeStruct((bsz, t_len, 2 * C_W), BF16)]
        out_specs = [io_spec, blk_spec(2 * C_W, 0)]
        scratch = [pltpu.VMEM((tb + 2 * HALO, 2 * C_W), F32)] + scratch
    return pl.pallas_call(
        functools.partial(_mlstm_kernel, reverse=reverse, final=final, tb=tb, nt=nt),
        out_shape=out_shape,
        grid=(bsz, nt),
        in_specs=in_specs,
        out_specs=out_specs,
        scratch_shapes=scratch,
        compiler_params=pltpu.CompilerParams(dimension_semantics=("parallel", "arbitrary"),
                                             vmem_limit_bytes=VMEM_LIMIT),
        name="mlstm_bwd_final" if final else "mlstm_fwd",
    )(*args)


def mlstm_mix(pc, p, tb):
    fwd_out = mlstm_pass(pc, p, None, reverse=False, tb=tb)
    return mlstm_pass(pc, p, fwd_out, reverse=True, tb=tb)


def _pad_cols(w, width):
    return jnp.pad(w, ((0, 0), (0, width - w.shape[1])))


def encoder_trunk(x, p, tb=512):
    bsz, t_len, d = x.shape
    n = bsz * t_len
    x2 = x.reshape(n, d)

    w_in = jnp.concatenate([p['even_w_in'][:, :A_IN], _pad_cols(p['even_w_in'][:, A_IN:], B_IN_PAD)], axis=1)
    pa, pb = norm_matmul(x2, p['norm_mix_g'][0], w_in.astype(BF16), (A_IN, B_IN_PAD))
    ya = rwkv_mix(pa.reshape(bsz, t_len, A_IN), p['rwkv'], tb)
    yb = gdn_mix(pb.reshape(bsz, t_len, B_IN_PAD), p['gdn'], tb)
    w_out = p['even_w_out'].astype(BF16)
    x2 = matmul_residual([ya.reshape(n, A_W), yb.reshape(n, B_W)], [w_out[:A_W], w_out[A_W:]], x2)
    x2 = ffn_residual(x2, p['norm_ffn_g'][0], p['ffn_w1'].astype(BF16), p['ffn_w3'].astype(BF16),
                      p['ffn_w2'].astype(BF16))

    w_odd = p['odd_w_in']
    n_gate = 2 * C_HEADS
    w_in = jnp.concatenate([w_odd[:, :4 * C_W], _pad_cols(w_odd[:, 4 * C_W:4 * C_W + n_gate], LANES),
                            _pad_cols(w_odd[:, 4 * C_W + n_gate:], LANES)], axis=1)
    (pc,) = norm_matmul(x2, p['norm_mix_g'][1], w_in.astype(BF16), (C_IN_PAD,))
    yc = mlstm_mix(pc.reshape(bsz, t_len, C_IN_PAD), p['mlstm'], tb)
    x2 = matmul_residual([yc.reshape(n, C_W)], [p['odd_w_out'].astype(BF16)], x2)
    x2 = moe_routed_final(x2, p['norm_ffn_g'][1], p['moe_router_w'], p['moe_router_b'],
                          p['moe_w1'].astype(BF16), p['moe_w3'].astype(BF16), p['moe_w2'].astype(BF16),
                          p['norm_final_g'])
    return x2.reshape(bsz, t_len, d)


def kernel(x_prompt, x_sample, norm_mix_g, norm_ffn_g, norm_final_g, even_w_in, rwkv_shift_mu, rwkv_w0, rwkv_w_up, rwkv_a0, rwkv_a_up, rwkv_g_up, rwkv_k_k, rwkv_k_a, rwkv_r_k, rwkv_ln_g, rwkv_ln_b, gdn_conv_w, gdn_A_log, gdn_dt_bias, gdn_norm_g, even_w_out, odd_w_in, mlstm_conv_w, mlstm_gate_b, mlstm_norm_g, odd_w_out, ffn_w1, ffn_w3, ffn_w2, moe_router_w, moe_router_b, moe_w1, moe_w3, moe_w2):
    p = dict(
        norm_mix_g=norm_mix_g, norm_ffn_g=norm_ffn_g, norm_final_g=norm_final_g,
        even_w_in=even_w_in[0], even_w_out=even_w_out[0], odd_w_in=odd_w_in[0], odd_w_out=odd_w_out[0],
        ffn_w1=ffn_w1[0], ffn_w3=ffn_w3[0], ffn_w2=ffn_w2[0],
        moe_router_w=moe_router_w[0], moe_router_b=moe_router_b[0],
        moe_w1=moe_w1[0], moe_w3=moe_w3[0], moe_w2=moe_w2[0],
        rwkv=dict(shift_mu=rwkv_shift_mu[0], w0=rwkv_w0[0], w_up=rwkv_w_up[0], a0=rwkv_a0[0], a_up=rwkv_a_up[0],
                  g_up=rwkv_g_up[0], k_k=rwkv_k_k[0], k_a=rwkv_k_a[0], r_k=rwkv_r_k[0], ln_g=rwkv_ln_g[0],
                  ln_b=rwkv_ln_b[0]),
        gdn=dict(conv_w=gdn_conv_w[0], A_log=gdn_A_log[0], dt_bias=gdn_dt_bias[0], norm_g=gdn_norm_g[0]),
        mlstm=dict(conv_w=mlstm_conv_w[0], gate_b=mlstm_gate_b[0], norm_g=mlstm_norm_g[0]),
    )
    return (encoder_trunk(x_prompt, p), encoder_trunk(x_sample, p))
```

```python
import functools

import jax
import jax.numpy as jnp
from jax import lax
from jax.experimental import pallas as pl
from jax.experimental.pallas import tpu as pltpu

F32 = jnp.float32
BF16 = jnp.bfloat16

D_MODEL = 1024
HEAD_DIM = 64
A_HEADS = 8
A_W = 512
W_RANK = 64
ICL_RANK = 64
G_RANK = 128
DECAY_SCALE = 0.606531
GN_EPS = 64e-5
A_IN = 1792
B_HEADS = 8
B_W = 512
C_HEADS = 8
C_HEAD_DIM = 128
C_W = 1024
CONV_W = 5
CHUNK = 64
D_FF = 2816
N_EXPERTS = 8
D_FF_EXPERT = 1408
HALO = 8
LANES = 128
PAIR = 2 * HEAD_DIM
N_PAIRS = A_W // PAIR
CHUNK_GROUP = 4
B_IN_PAD = 4 * B_W + LANES
C_IN_PAD = 4 * C_W + 2 * LANES

VMEM_LIMIT = 56 * 1024 * 1024

NT_DIMS = (((1,), (1,)), ((), ()))
TN_DIMS = (((0,), (0,)), ((), ()))


def _dot(a, b):
    return jnp.dot(a.astype(BF16), b.astype(BF16), preferred_element_type=F32)


def _dot_nt(a, b):
    return lax.dot_general(a.astype(BF16), b.astype(BF16), NT_DIMS, preferred_element_type=F32)


def _dot_tn(a, b):
    return lax.dot_general(a.astype(BF16), b.astype(BF16), TN_DIMS, preferred_element_type=F32)


def _split3(x):
    h = x.astype(BF16)
    r = x - h.astype(F32)
    m = r.astype(BF16)
    l = (r - m.astype(F32)).astype(BF16)
    return h, m, l


def _dot_exact_rhs(x, c):
    h, m, l = _split3(x)
    return (jnp.dot(h, c, preferred_element_type=F32) + jnp.dot(m, c, preferred_element_type=F32)
            + jnp.dot(l, c, preferred_element_type=F32))


def _dot_rhs01(x, c):
    return jnp.dot(x.astype(BF16), c, preferred_element_type=F32)


def _dot_exact_lhs(c, x):
    h, m, l = _split3(x)
    return (jnp.dot(c, h, preferred_element_type=F32) + jnp.dot(c, m, preferred_element_type=F32)
            + jnp.dot(c, l, preferred_element_type=F32))


def _dot_tn_exact_rhs(x, c):
    h, m, l = _split3(x)
    f = lambda p: lax.dot_general(p, c, TN_DIMS, preferred_element_type=F32)
    return f(h) + f(m) + f(l)


def _dot3(a, b):
    ah = a.astype(BF16)
    al = (a - ah.astype(F32)).astype(BF16)
    bh = b.astype(BF16)
    bl = (b - bh.astype(F32)).astype(BF16)
    f = lambda p, q: jnp.dot(p, q, preferred_element_type=F32)
    return f(ah, bh) + f(ah, bl) + f(al, bh)


def _sigmoid(x):
    return 1.0 / (1.0 + jnp.exp(-x))


def _silu(x):
    return x * _sigmoid(x)


def _softplus(x):
    return jnp.maximum(x, 0.0) + jnp.log(1.0 + jnp.exp(-jnp.abs(x)))


def _norm_matmul_kernel(x_ref, g_ref, w_ref, *refs, widths, tn, with_rows):
    x = x_ref[...]
    h = (x * lax.rsqrt(jnp.mean(x * x, axis=-1, keepdims=True) + 1e-6) * g_ref[...]).astype(BF16)
    if with_rows:
        wt_ref, refs, ot_ref = refs[0], refs[1:-1], refs[-1]
        ot_ref[...] = lax.dot_general(wt_ref[...], h, NT_DIMS, preferred_element_type=F32)
    off = 0
    for o_ref, wd in zip(refs, widths):
        for c in range(0, wd, tn):
            cw = min(tn, wd - c)
            o_ref[:, c:c + cw] = jnp.dot(h, w_ref[:, off + c:off + c + cw], preferred_element_type=F32)
        off += wd


def norm_matmul(x2d, g, w, widths, w_rows=None, tm=256, tn=512):
    n, d = x2d.shape
    ntot = w.shape[1]
    assert sum(widths) == ntot and n % tm == 0
    args = [x2d, g.reshape(1, d), w]
    in_specs = [pl.BlockSpec((tm, d), lambda i: (i, 0)),
                pl.BlockSpec((1, d), lambda i: (0, 0)),
                pl.BlockSpec((d, ntot), lambda i: (0, 0))]
    out_shape = [jax.ShapeDtypeStruct((n, wd), F32) for wd in widths]
    out_specs = [pl.BlockSpec((tm, wd), lambda i: (i, 0)) for wd in widths]
    if w_rows is not None:
        args.append(w_rows)
        in_specs.append(pl.BlockSpec(w_rows.shape, lambda i: (0, 0)))
        out_shape.append(jax.ShapeDtypeStruct((w_rows.shape[0], n), F32))
        out_specs.append(pl.BlockSpec((w_rows.shape[0], tm), lambda i: (0, i)))
    return pl.pallas_call(
        functools.partial(_norm_matmul_kernel, widths=widths, tn=tn, with_rows=w_rows is not None),
        out_shape=out_shape,
        grid=(n // tm,),
        in_specs=in_specs,
        out_specs=out_specs,
        compiler_params=pltpu.CompilerParams(dimension_semantics=("parallel",), vmem_limit_bytes=VMEM_LIMIT),
        name="norm_matmul",
    )(*args)


def _matmul_res_kernel(*refs, n_in):
    y_refs = refs[:n_in]
    w_refs = refs[n_in:2 * n_in]
    x_ref, o_ref = refs[2 * n_in], refs[2 * n_in + 1]
    acc = x_ref[...]
    for y_ref, w_ref in zip(y_refs, w_refs):
        acc = acc + jnp.dot(y_ref[...], w_ref[...], preferred_element_type=F32)
    o_ref[...] = acc


def matmul_residual(ys, ws, x2d, tm=512):
    n, d = x2d.shape
    assert n % tm == 0
    return pl.pallas_call(
        functools.partial(_matmul_res_kernel, n_in=len(ys)),
        out_shape=jax.ShapeDtypeStruct((n, d), F32),
        grid=(n // tm,),
        in_specs=([pl.BlockSpec((tm, y.shape[1]), lambda i: (i, 0)) for y in ys]
                  + [pl.BlockSpec(w.shape, lambda i: (0, 0)) for w in ws]
                  + [pl.BlockSpec((tm, d), lambda i: (i, 0))]),
        out_specs=pl.BlockSpec((tm, d), lambda i: (i, 0)),
        compiler_params=pltpu.CompilerParams(dimension_semantics=("parallel",), vmem_limit_bytes=VMEM_LIMIT),
        name="matmul_residual",
    )(*ys, *ws, x2d)


def _ff_chunks(width, tf):
    return [(c, min(tf, width - c)) for c in range(0, width, tf)]


def _ffn_kernel(x_ref, g_ref, w1_ref, w3_ref, w2_ref, o_ref, *, tf):
    x = x_ref[...]
    h = (x * lax.rsqrt(jnp.mean(x * x, axis=-1, keepdims=True) + 1e-6) * g_ref[...]).astype(BF16)
    acc = x
    for c, cw in _ff_chunks(D_FF, tf):
        a = jnp.dot(h, w1_ref[:, c:c + cw], preferred_element_type=F32)
        b = jnp.dot(h, w3_ref[:, c:c + cw], preferred_element_type=F32)
        acc = acc + jnp.dot((_silu(a) * b).astype(BF16), w2_ref[c:c + cw, :], preferred_element_type=F32)
    o_ref[...] = acc


def ffn_residual(x2d, g, w1, w3, w2, tm=512, tf=256):
    n, d = x2d.shape
    assert n % tm == 0
    const = lambda shape: pl.BlockSpec(shape, lambda i: (0, 0), pipeline_mode=pl.Buffered(1))
    return pl.pallas_call(
        functools.partial(_ffn_kernel, tf=tf),
        out_shape=jax.ShapeDtypeStruct((n, d), F32),
        grid=(n // tm,),
        in_specs=[pl.BlockSpec((tm, d), lambda i: (i, 0)), const((1, d)),
                  const(w1.shape), const(w3.shape), const(w2.shape)],
        out_specs=pl.BlockSpec((tm, d), lambda i: (i, 0)),
        compiler_params=pltpu.CompilerParams(dimension_semantics=("parallel",), vmem_limit_bytes=VMEM_LIMIT),
        name="ffn_residual",
    )(x2d, g.reshape(1, d), w1, w3, w2)


MOE_TM = 256
MOE_TG = 512
ROW_ALIGN = 16


def _router_kernel(x_ref, g_ref, rw_ref, rb_ref, h_ref, gate_ref, sel_ref, cnt_ref):
    x = x_ref[...]
    hf = x * lax.rsqrt(jnp.mean(x * x, axis=-1, keepdims=True) + 1e-6) * g_ref[...]
    h_ref[...] = hf.astype(BF16)
    logits = _dot3(hf, rw_ref[...]) + rb_ref[...]
    lane = lax.broadcasted_iota(jnp.int32, logits.shape, 1)
    m1 = jnp.max(logits, axis=-1, keepdims=True)
    i1 = jnp.min(jnp.where(logits == m1, lane, LANES), axis=-1, keepdims=True)
    rest = jnp.where(lane == i1, -jnp.inf, logits)
    m2 = jnp.max(rest, axis=-1, keepdims=True)
    i2 = jnp.min(jnp.where(rest == m2, lane, LANES), axis=-1, keepdims=True)
    e2 = jnp.exp(m2 - m1)
    den = 1.0 + e2
    gate_ref[...] = jnp.where(lane == i1, 1.0 / den, jnp.where(lane == i2, e2 / den, 0.0))
    sel = jnp.where((lane == i1) | (lane == i2), 1.0, 0.0)
    sel_ref[...] = sel.astype(BF16)
    cnt_ref[...] = jnp.sum(sel, axis=0, keepdims=True)


def _dispatch_kernel(base_ref, h_ref, sel_ref, upper_ref, eye_ref, xs_in_ref, xs_ref, cbuf, sem, *, tm, n_tiles):
    del xs_in_ref
    s = pl.program_id(0)
    slot = s % 2
    sel = sel_ref[...]
    rank_t = lax.dot_general(sel, upper_ref[...], TN_DIMS, preferred_element_type=F32)
    sel_t = lax.dot_general(sel, eye_ref[...], TN_DIMS, preferred_element_type=F32)
    row = lax.broadcasted_iota(jnp.int32, (tm, tm), 0).astype(F32)
    h = h_ref[...]
    for e in range(N_EXPERTS):
        pick = jnp.where((rank_t[e:e + 1, :] == row) & (sel_t[e:e + 1, :] > 0.5), 1.0, 0.0).astype(BF16)
        cbuf[slot, e] = jnp.dot(pick, h, preferred_element_type=F32).astype(BF16)

    def copy(sl, e, tile):
        start = pl.multiple_of(base_ref[tile * N_EXPERTS + e], ROW_ALIGN)
        return pltpu.make_async_copy(cbuf.at[sl, e], xs_ref.at[pl.ds(start, tm)], sem.at[sl, e])

    @pl.when(s > 0)
    def _():
        for e in range(N_EXPERTS):
            copy(1 - slot, e, s - 1).wait()

    for e in range(N_EXPERTS):
        copy(slot, e, s).start()

    @pl.when(s == n_tiles - 1)
    def _():
        for e in range(N_EXPERTS):
            copy(slot, e, s).wait()


def _grouped_ffn_kernel(te_ref, tv_ref, x_ref, w1_ref, w3_ref, w2_ref, o_ref, *, tf):
    del te_ref
    valid = tv_ref[pl.program_id(0)] == 1

    @pl.when(jnp.logical_not(valid))
    def _():
        o_ref[...] = jnp.zeros(o_ref.shape, o_ref.dtype)

    @pl.when(valid)
    def _():
        h = x_ref[...]
        y = jnp.zeros(o_ref.shape, F32)
        for c, cw in _ff_chunks(D_FF_EXPERT, tf):
            a = jnp.dot(h, w1_ref[:, c:c + cw], preferred_element_type=F32)
            b = jnp.dot(h, w3_ref[:, c:c + cw], preferred_element_type=F32)
            y = y + jnp.dot((_silu(a) * b).astype(BF16), w2_ref[c:c + cw, :], preferred_element_type=F32)
        o_ref[...] = y.astype(o_ref.dtype)


def _combine_kernel(base_ref, x_ref, sel_ref, gate_ref, lower_ref, gf_ref, ys_ref, o_ref, ybuf, sem, *, tm, n_tiles):
    s = pl.program_id(0)
    slot = s % 2

    def copy(sl, e, tile):
        start = pl.multiple_of(base_ref[tile * N_EXPERTS + e], ROW_ALIGN)
        return pltpu.make_async_copy(ys_ref.at[pl.ds(start, tm)], ybuf.at[sl, e], sem.at[sl, e])

    @pl.when(s == 0)
    def _():
        for e in range(N_EXPERTS):
            copy(slot, e, s).start()

    @pl.when(s + 1 < n_tiles)
    def _():
        for e in range(N_EXPERTS):
            copy(1 - slot, e, s + 1).start()

    sel = sel_ref[...]
    rank = jnp.dot(lower_ref[...], sel, preferred_element_type=F32)
    self32 = sel.astype(F32)
    gate = gate_ref[...]
    col = lax.broadcasted_iota(jnp.int32, (tm, tm), 1).astype(F32)
    for e in range(N_EXPERTS):
        copy(slot, e, s).wait()
    acc = x_ref[...]
    for e in range(N_EXPERTS):
        pick = jnp.where((rank[:, e:e + 1] == col) & (self32[:, e:e + 1] > 0.5), 1.0, 0.0).astype(BF16)
        acc = acc + gate[:, e:e + 1] * jnp.dot(pick, ybuf[slot, e], preferred_element_type=F32)
    o_ref[...] = acc * lax.rsqrt(jnp.mean(acc * acc, axis=-1, keepdims=True) + 1e-6) * gf_ref[...]


def moe_routed_final(x2d, g, router_w, router_b, w1, w3, w2, g_final, tf=256):
    n, d = x2d.shape
    tm, tg = MOE_TM, MOE_TG
    assert n % tm == 0
    n_tiles = n // tm
    rw = jnp.zeros((d, LANES), F32).at[:, :N_EXPERTS].set(router_w)
    rb = jnp.full((1, LANES), -1e30, F32).at[0, :N_EXPERTS].set(router_b)
    tile = lambda width: pl.BlockSpec((tm, width), lambda i: (i, 0))
    const = lambda shape: pl.BlockSpec(shape, lambda i: (0,) * len(shape))
    h, gates, sel, cnt = pl.pallas_call(
        _router_kernel,
        out_shape=[jax.ShapeDtypeStruct((n, d), BF16), jax.ShapeDtypeStruct((n, LANES), F32),
                   jax.ShapeDtypeStruct((n, LANES), BF16), jax.ShapeDtypeStruct((n_tiles, 1, LANES), F32)],
        grid=(n_tiles,),
        in_specs=[tile(d), const((1, d)), const((d, LANES)), const((1, LANES))],
        out_specs=[tile(d), tile(LANES), tile(LANES), pl.BlockSpec((None, 1, LANES), lambda i: (i, 0, 0))],
        compiler_params=pltpu.CompilerParams(dimension_semantics=("parallel",), vmem_limit_bytes=VMEM_LIMIT),
        name="moe_router",
    )(x2d, g.reshape(1, d), rw, rb)

    cnt = cnt[:, 0, :N_EXPERTS].astype(jnp.int32)
    padded = (cnt + ROW_ALIGN - 1) // ROW_ALIGN * ROW_ALIGN
    total = jnp.sum(padded, axis=0)
    written = total + tm
    size = (written + tg - 1) // tg * tg
    seg_end = jnp.cumsum(size)
    offs = seg_end - size
    base = (offs[None, :] + jnp.cumsum(padded, axis=0) - padded).reshape(-1).astype(jnp.int32)
    p_rows = 2 * n + n_tiles * N_EXPERTS * ROW_ALIGN + N_EXPERTS * (tm + tg)
    p_rows = (p_rows + tg - 1) // tg * tg
    n_gt = p_rows // tg
    tile_start = jnp.arange(n_gt, dtype=jnp.int32) * tg
    tile_expert = jnp.minimum(jnp.sum(tile_start[:, None] >= seg_end[None, :], axis=1), N_EXPERTS - 1).astype(jnp.int32)
    tile_valid = (tile_start < (offs + written)[tile_expert]).astype(jnp.int32)

    r = jnp.arange(tm)
    upper = (r[:, None] < r[None, :]).astype(BF16)
    eye = (r[:, None] == r[None, :]).astype(BF16)
    xs = pl.pallas_call(
        functools.partial(_dispatch_kernel, tm=tm, n_tiles=n_tiles),
        out_shape=jax.ShapeDtypeStruct((p_rows, d), BF16),
        grid_spec=pltpu.PrefetchScalarGridSpec(
            num_scalar_prefetch=1, grid=(n_tiles,),
            in_specs=[pl.BlockSpec((tm, d), lambda i, b: (i, 0)), pl.BlockSpec((tm, LANES), lambda i, b: (i, 0)),
                      pl.BlockSpec((tm, tm), lambda i, b: (0, 0)), pl.BlockSpec((tm, tm), lambda i, b: (0, 0)),
                      pl.BlockSpec(memory_space=pl.ANY)],
            out_specs=pl.BlockSpec(memory_space=pl.ANY),
            scratch_shapes=[pltpu.VMEM((2, N_EXPERTS, tm, d), BF16), pltpu.SemaphoreType.DMA((2, N_EXPERTS))]),
        input_output_aliases={5: 0},
        compiler_params=pltpu.CompilerParams(dimension_semantics=("arbitrary",), vmem_limit_bytes=VMEM_LIMIT),
        name="moe_dispatch",
    )(base, h, sel, upper, eye, jnp.zeros((p_rows, d), BF16))

    ys = pl.pallas_call(
        functools.partial(_grouped_ffn_kernel, tf=tf),
        out_shape=jax.ShapeDtypeStruct((p_rows, d), BF16),
        grid_spec=pltpu.PrefetchScalarGridSpec(
            num_scalar_prefetch=2, grid=(n_gt,),
            in_specs=[pl.BlockSpec((tg, d), lambda j, te, tv: (j, 0)),
                      pl.BlockSpec((None, d, D_FF_EXPERT), lambda j, te, tv: (te[j], 0, 0)),
                      pl.BlockSpec((None, d, D_FF_EXPERT), lambda j, te, tv: (te[j], 0, 0)),
                      pl.BlockSpec((None, D_FF_EXPERT, d), lambda j, te, tv: (te[j], 0, 0))],
            out_specs=pl.BlockSpec((tg, d), lambda j, te, tv: (j, 0))),
        compiler_params=pltpu.CompilerParams(dimension_semantics=("arbitrary",), vmem_limit_bytes=VMEM_LIMIT),
        name="moe_grouped_ffn",
    )(tile_expert, tile_valid, xs, w1, w3, w2)

    lower = (r[:, None] > r[None, :]).astype(BF16)
    return pl.pallas_call(
        functools.partial(_combine_kernel, tm=tm, n_tiles=n_tiles),
        out_shape=jax.ShapeDtypeStruct((n, d), F32),
        grid_spec=pltpu.PrefetchScalarGridSpec(
            num_scalar_prefetch=1, grid=(n_tiles,),
            in_specs=[pl.BlockSpec((tm, d), lambda i, b: (i, 0)), pl.BlockSpec((tm, LANES), lambda i, b: (i, 0)),
                      pl.BlockSpec((tm, LANES), lambda i, b: (i, 0)), pl.BlockSpec((tm, tm), lambda i, b: (0, 0)),
                      pl.BlockSpec((1, d), lambda i, b: (0, 0)), pl.BlockSpec(memory_space=pl.ANY)],
            out_specs=pl.BlockSpec((tm, d), lambda i, b: (i, 0)),
            scratch_shapes=[pltpu.VMEM((2, N_EXPERTS, tm, d), BF16), pltpu.SemaphoreType.DMA((2, N_EXPERTS))]),
        compiler_params=pltpu.CompilerParams(dimension_semantics=("arbitrary",), vmem_limit_bytes=VMEM_LIMIT),
        name="moe_combine",
    )(base, x2d, sel, gates, lower, g_final.reshape(1, d), ys)


def _seq_specs(tb, nt, width, reverse, col_block=0):
    hb = tb // HALO
    nhb = nt * hb
    pos = (lambda t: nt - 1 - t) if reverse else (lambda t: t)
    main = pl.BlockSpec((None, tb, width), lambda b, t: (b, pos(t), col_block))
    prev = pl.BlockSpec((None, HALO, width), lambda b, t: (b, jnp.maximum(pos(t) * hb - 1, 0), col_block))
    nxt = pl.BlockSpec((None, HALO, width), lambda b, t: (b, jnp.minimum((pos(t) + 1) * hb, nhb - 1), col_block))
    return main, prev, nxt


def _fill_extended(xe_ref, x_ref, xp_ref, xn_ref, pos, nt, tb, width=None):
    sl = slice(None) if width is None else slice(0, width)
    xe_ref[0:HALO, :] = jnp.where(pos == 0, 0.0, xp_ref[:, sl])
    xe_ref[HALO:HALO + tb, :] = x_ref[:, sl]
    xe_ref[HALO + tb:2 * HALO + tb, :] = jnp.where(pos == nt - 1, 0.0, xn_ref[:, sl])


def _seg_mats():
    r = lax.broadcasted_iota(jnp.int32, (A_W, A_W), 0) // HEAD_DIM
    c = lax.broadcasted_iota(jnp.int32, (A_W, A_W), 1) // HEAD_DIM
    return (r == c).astype(BF16)


def _tri_const(reverse, reps=1):
    r = jnp.arange(CHUNK)[:, None]
    c = jnp.arange(CHUNK)[None, :]
    m = ((c >= r) if reverse else (c <= r)).astype(BF16)
    return jnp.tile(m, (1, reps))


def _pair_masks(reverse):
    row = lax.broadcasted_iota(jnp.int32, (CHUNK, PAIR), 0)
    lane = lax.broadcasted_iota(jnp.int32, (CHUNK, PAIR), 1)
    col = lane % HEAD_DIM
    strict = (col > row) if reverse else (col < row)
    incl = (col >= row) if reverse else (col <= row)
    eye = (col == row).astype(F32)
    r2 = lax.broadcasted_iota(jnp.int32, (PAIR, PAIR), 0) // HEAD_DIM
    c2 = lax.broadcasted_iota(jnp.int32, (PAIR, PAIR), 1) // HEAD_DIM
    same_head = r2 == c2
    return strict, incl, eye, same_head.astype(BF16), lane < HEAD_DIM, same_head


def _bd(x2, bdm):
    xb = x2.astype(BF16)
    return jnp.concatenate([xb, xb], axis=0) * bdm


def _mm(a, b):
    return jnp.dot(a.astype(BF16), b, preferred_element_type=F32)


def _tri_inverse_levels(n2s, eye2, bdm, out):
    ps = [_mm(n, _bd(n, bdm)) for n in n2s]
    ts = [eye2 + n for n in n2s]
    yield
    for i in range(5):
        if i < 4:
            outs = [_mm(p, jnp.concatenate([_bd(t, bdm), _bd(p, bdm)], axis=1)) for t, p in zip(ts, ps)]
            ps = [o[:, PAIR:] for o in outs]
        else:
            outs = [_mm(p, _bd(t, bdm)) for t, p in zip(ts, ps)]
        ts = [t + o[:, 0:PAIR] for t, o in zip(ts, outs)]
        yield
    out.extend(ts)


def _run_interleaved(*gens):
    live = [g for g in gens if g is not None]
    while live:
        for g in list(live):
            try:
                next(g)
            except StopIteration:
                live.remove(g)


def _software_pipeline(n_groups, phase_a, chain):
    _run_interleaved(phase_a(0))
    for g in range(n_groups):
        _run_interleaved(chain(g), phase_a(g + 1) if g + 1 < n_groups else None)


def _rwkv_kernel(*refs, reverse, final, tb, nt, group):
    if final:
        (x_ref, xp_ref, xn_ref, mu_ref, kk_ref, ka_ref, w0_ref, wup_ref, a0_ref, aup_ref, seg_ref, tri_ref,
         gup_ref, rk_ref, lng_ref, lnb_ref, yf_ref, o_ref,
         xe_ref, s_ref, r_s, k_s, v_s, a_s, b_s, lw_s, y_s) = refs
    else:
        (x_ref, xp_ref, xn_ref, mu_ref, kk_ref, ka_ref, w0_ref, wup_ref, a0_ref, aup_ref, seg_ref, tri_ref,
         o_ref, xe_ref, s_ref, r_s, k_s, v_s, a_s, b_s, lw_s, y_s) = refs
    d = 1 if reverse else 0
    t = pl.program_id(1)
    pos = (nt - 1 - t) if reverse else t
    nc = tb // CHUNK

    @pl.when(t == 0)
    def _():
        s_ref[...] = jnp.zeros(s_ref.shape, F32)

    _fill_extended(xe_ref, x_ref, xp_ref, xn_ref, pos, nt, tb)
    x = x_ref[...]
    nb = 0.5 * (xe_ref[pl.ds(HALO - 1, tb), :] + xe_ref[pl.ds(HALO + 1, tb), :])
    pa = x + mu_ref[...] * (nb - x)
    r = pa[:, 0:A_W]
    k = pa[:, A_W:2 * A_W]
    v = pa[:, 2 * A_W:3 * A_W]
    wd = jnp.tanh(pa[:, 3 * A_W:3 * A_W + W_RANK])
    ad = pa[:, 3 * A_W + W_RANK:3 * A_W + W_RANK + ICL_RANK]
    seg = seg_ref[...]
    kq = k * kk_ref[...]
    kk = kq * lax.rsqrt(_dot_rhs01(kq * kq, seg) + 1e-6)
    lw = -DECAY_SCALE * _sigmoid(w0_ref[d:d + 1, :] + _dot(wd, wup_ref[d]))
    icl = _sigmoid(a0_ref[d:d + 1, :] + _dot(ad, aup_ref[d]))
    ka = ka_ref[...]
    r_s[...] = r
    k_s[...] = k * (1.0 + (icl - 1.0) * ka)
    v_s[...] = v
    a_s[...] = -kk
    b_s[...] = kk * icl
    lw_s[...] = lw

    strict, incl, eye2, bdm, _, same_head = _pair_masks(reverse)
    tri = tri_ref[...]

    groups = [list(range(nc))[j:j + group] for j in range(0, nc, group)]
    if reverse:
        groups = [[nc - 1 - c for c in g] for g in groups]
    staged = {}

    def phase_a(g):
        probs = []
        for c in groups[g]:
            rows = slice(c * CHUNK, (c + 1) * CHUNK)
            lwc = lw_s[rows, :]
            cum = _dot_exact_lhs(tri, lwc)
            last = cum[0:1, :] if reverse else cum[CHUNK - 1:CHUNK, :]
            rt = r_s[rows, :] * jnp.exp(cum)
            at = a_s[rows, :] * jnp.exp(cum - lwc)
            einv = jnp.exp(-cum)
            bc = b_s[rows, :]
            kc = k_s[rows, :]
            bt = bc * einv
            kt = kc * einv
            eend = jnp.exp(last - cum)
            bh = bc * eend
            kh = kc * eend
            gl = jnp.exp(last)
            vc = v_s[rows, :]
            for p in range(N_PAIRS):
                sl = slice(p * PAIR, (p + 1) * PAIR)
                probs.append(dict(c=c, p=p, at=at[:, sl], rt=rt[:, sl], bt=bt[:, sl], kt=kt[:, sl],
                                  bk=jnp.concatenate([bh[:, sl], kh[:, sl]], axis=0).astype(BF16),
                                  gl=gl[:, sl], v=vc[:, sl]))
        yield
        for q in probs:
            xm = jnp.concatenate([q['at'], q['rt']], axis=0).astype(BF16)
            wm = jnp.concatenate([_bd(q.pop('bt'), bdm), _bd(q.pop('kt'), bdm)], axis=0)
            pm = lax.dot_general(xm, wm, NT_DIMS, preferred_element_type=F32)
            q['a_ab'] = jnp.where(strict, pm[0:CHUNK, 0:PAIR], 0.0)
            q['a_kr'] = jnp.concatenate([jnp.where(strict, pm[0:CHUNK, PAIR:], 0.0),
                                         jnp.where(incl, pm[CHUNK:, PAIR:], 0.0)], axis=0).astype(BF16)
            q['a_rb'] = jnp.where(incl, pm[CHUNK:, 0:PAIR], 0.0).astype(BF16)
        yield
        tinvs = []
        yield from _tri_inverse_levels([q.pop('a_ab') for q in probs], eye2, bdm, tinvs)
        for q in probs:
            q['wy'] = _mm(q.pop('a_kr'), _bd(q['v'], bdm))
        yield
        for q, tinv in zip(probs, tinvs):
            wy = q.pop('wy')
            ua = _mm(tinv, jnp.concatenate([_bd(wy[0:CHUNK], bdm), _bd(q.pop('at'), bdm)], axis=1))
            q['u0'] = ua[:, 0:PAIR]
            q['ar'] = jnp.concatenate([ua[:, PAIR:], q.pop('rt')], axis=0).astype(BF16)
            q['y0'] = wy[CHUNK:]
        yield
        staged[g] = probs

    def chain(g):
        probs = staged.pop(g)
        for j in range(0, len(probs), N_PAIRS):
            cp = probs[j:j + N_PAIRS]
            ss = [s_ref[q['p']] for q in cp]
            uys = [lax.dot_general(q['ar'], s.astype(BF16), NT_DIMS, preferred_element_type=F32)
                   for q, s in zip(cp, ss)]
            yield
            ys = []
            for q, s, uy in zip(cp, ss, uys):
                u = uy[0:CHUNK] + q['u0']
                ys.append(uy[CHUNK:] + _mm(q['a_rb'], _bd(u, bdm)) + q['y0'])
                upd = lax.dot_general(jnp.concatenate([u, q['v']], axis=0).astype(BF16), q['bk'], TN_DIMS,
                                      preferred_element_type=F32)
                s_ref[q['p']] = s * q['gl'] + jnp.where(same_head, upd, 0.0)
            c = cp[0]['c']
            y_s[c * CHUNK:(c + 1) * CHUNK, :] = jnp.concatenate(ys, axis=1)
            yield

    _software_pipeline(len(groups), phase_a, chain)

    if not final:
        o_ref[...] = y_s[...]
    else:
        y = yf_ref[...] + y_s[...]
        seg_mean = lambda z: _dot_rhs01(z, seg) * (1.0 / HEAD_DIM)
        mu = seg_mean(y)
        yc = y - mu
        var = seg_mean(yc * yc)
        yn = yc * lax.rsqrt(var + GN_EPS) * lng_ref[...] + lnb_ref[...]
        o = 1 - d
        icl_o = _sigmoid(a0_ref[o:o + 1, :] + _dot(ad, aup_ref[o]))
        ksum = k * (2.0 + (icl + icl_o - 2.0) * ka)
        bonus = _dot_rhs01(r * ksum * rk_ref[...], seg) * v
        gate = _dot(_sigmoid(pa[:, 3 * A_W + W_RANK + ICL_RANK:]), gup_ref[...])
        o_ref[...] = ((yn + bonus) * gate).astype(o_ref.dtype)


def rwkv_pass(pa, p, y_fwd, *, reverse, tb):
    bsz, t_len, _ = pa.shape
    nt = t_len // tb
    final = y_fwd is not None
    main, prev, nxt = _seq_specs(tb, nt, A_IN, reverse)
    const2 = lambda a: pl.BlockSpec(a.shape, lambda b, t: (0,) * a.ndim)
    row = lambda a: a.reshape(1, -1).astype(F32)
    consts = [row(p['shift_mu']), row(p['k_k']), row(p['k_a']), p['w0'].astype(F32), p['w_up'].astype(BF16),
              p['a0'].astype(F32), p['a_up'].astype(BF16), _seg_mats(), _tri_const(reverse)]
    args = [pa, pa, pa] + consts
    in_specs = [main, prev, nxt] + [const2(a) for a in consts]
    pos = (lambda t: nt - 1 - t) if reverse else (lambda t: t)
    io_spec = pl.BlockSpec((None, tb, A_W), lambda b, t: (b, pos(t), 0))
    if final:
        extra = [p['g_up'].astype(BF16), row(p['r_k']), row(p['ln_g']), row(p['ln_b'])]
        args += extra + [y_fwd]
        in_specs += [const2(a) for a in extra] + [io_spec]
    blk = lambda: pltpu.VMEM((tb, A_W), F32)
    return pl.pallas_call(
        functools.partial(_rwkv_kernel, reverse=reverse, final=final, tb=tb, nt=nt, group=CHUNK_GROUP),
        out_shape=jax.ShapeDtypeStruct((bsz, t_len, A_W), BF16 if final else F32),
        grid=(bsz, nt),
        in_specs=in_specs,
        out_specs=io_spec,
        scratch_shapes=[pltpu.VMEM((tb + 2 * HALO, A_IN), F32), pltpu.VMEM((N_PAIRS, PAIR, PAIR), F32),
                        blk(), blk(), blk(), blk(), blk(), blk(), blk()],
        compiler_params=pltpu.CompilerParams(dimension_semantics=("parallel", "arbitrary"),
                                             vmem_limit_bytes=VMEM_LIMIT),
        name="rwkv_bwd_final" if final else "rwkv_fwd",
    )(*args)


def rwkv_mix(pa, p, tb):
    y_f = rwkv_pass(pa, p, None, reverse=False, tb=tb)
    return rwkv_pass(pa, p, y_f, reverse=True, tb=tb)


def _gdn_kernel(*refs, reverse, final, tb, nt, group):
    if final:
        (qkv_ref, z_ref, gl_ref, alog_ref, dtb_ref, seg_ref, tri_ref, trit_ref, exg_ref, exb_ref,
         ng_ref, of_ref, o_ref, s_ref, q_s, k_s, v_s, g_s, bx_s, y_s) = refs
    else:
        (x_ref, xp_ref, xn_ref, cw_ref, alog_ref, dtb_ref, seg_ref, tri_ref, trit_ref, exg_ref, exb_ref,
         o_ref, qkv_out_ref, xe_ref, s_ref, q_s, k_s, v_s, g_s, bx_s, y_s) = refs
    d = 1 if reverse else 0
    t = pl.program_id(1)
    pos = (nt - 1 - t) if reverse else t
    nc = tb // CHUNK
    qkv_w = 3 * B_W

    @pl.when(t == 0)
    def _():
        s_ref[...] = jnp.zeros(s_ref.shape, F32)

    seg = seg_ref[...]
    if final:
        q_s[...] = qkv_ref[:, 0:B_W].astype(F32)
        k_s[...] = qkv_ref[:, B_W:2 * B_W].astype(F32)
        v_s[...] = qkv_ref[:, 2 * B_W:].astype(F32)
        gl = gl_ref[...]
    else:
        _fill_extended(xe_ref, x_ref, xp_ref, xn_ref, pos, nt, tb, width=qkv_w)
        conv = xe_ref[pl.ds(HALO - 2, tb), :] * cw_ref[0:1, :]
        for j in range(1, CONV_W):
            conv = conv + xe_ref[pl.ds(HALO - 2 + j, tb), :] * cw_ref[j:j + 1, :]
        qkv = _silu(conv)
        q = qkv[:, 0:B_W]
        k = qkv[:, B_W:2 * B_W]
        qn = q * lax.rsqrt(_dot_rhs01(q * q, seg) + 1e-6) * (HEAD_DIM ** -0.5)
        kn = k * lax.rsqrt(_dot_rhs01(k * k, seg) + 1e-6)
        q_s[...] = qn
        k_s[...] = kn
        v_s[...] = qkv[:, 2 * B_W:]
        qkv_out_ref[...] = jnp.concatenate([qn, kn, qkv[:, 2 * B_W:]], axis=1).astype(BF16)
        gl = x_ref[:, 4 * B_W:4 * B_W + LANES]
    g_s[...] = -jnp.exp(alog_ref[...]) * _softplus(gl + dtb_ref[...])
    bx_s[...] = _dot_rhs01(_sigmoid(gl), exb_ref[...])

    strict, incl, eye2, bdm, left, same_head = _pair_masks(reverse)
    tri = tri_ref[...]
    trit2 = trit_ref[...]
    exg = exg_ref[...]

    groups = [list(range(nc))[j:j + group] for j in range(0, nc, group)]
    if reverse:
        groups = [[nc - 1 - c for c in g] for g in groups]
    staged = {}

    def phase_a(g):
        probs = []
        for c in groups[g]:
            rows = slice(c * CHUNK, (c + 1) * CHUNK)
            gch = g_s[rows, :]
            gc = _dot_exact_lhs(tri, gch)
            gct = _dot_tn_exact_rhs(gch, trit2)
            last = gc[0:1, :] if reverse else gc[CHUNK - 1:CHUNK, :]
            eg = _dot_rhs01(jnp.exp(gc), exg)
            eend = _dot_rhs01(jnp.exp(last - gc), exg)
            glast = _dot_exact_rhs(jnp.exp(last), exg)
            qc = q_s[rows, :]
            kc = k_s[rows, :]
            vc = v_s[rows, :]
            beta = bx_s[rows, :]
            kb = kc * beta
            vb = vc * beta
            kbe = kb * eg
            qd = qc * eg
            kd = kc * eend
            for p in range(N_PAIRS):
                sl = slice(p * PAIR, (p + 1) * PAIR)
                l0 = d * B_HEADS + 2 * p
                diff = (jnp.where(left, gc[:, l0:l0 + 1], gc[:, l0 + 1:l0 + 2])
                        - jnp.where(left[0:1], gct[l0:l0 + 1, :], gct[l0 + 1:l0 + 2, :]))
                probs.append(dict(c=c, p=p, decay=jnp.exp(jnp.where(incl, diff, 0.0)),
                                  kq=jnp.concatenate([kb[:, sl], qc[:, sl]], axis=0).astype(BF16), k=kc[:, sl],
                                  vb=vb[:, sl], kbe=kbe[:, sl], qd=qd[:, sl], kd=kd[:, sl].astype(BF16),
                                  gl=glast[:, sl]))
        yield
        for q in probs:
            pm = lax.dot_general(q.pop('kq'), _bd(q.pop('k'), bdm), NT_DIMS, preferred_element_type=F32)
            decay = q.pop('decay')
            q['n'] = jnp.where(strict, -pm[0:CHUNK] * decay, 0.0)
            q['qk'] = jnp.where(incl, pm[CHUNK:] * decay, 0.0).astype(BF16)
        yield
        tinvs = []
        yield from _tri_inverse_levels([q.pop('n') for q in probs], eye2, bdm, tinvs)
        for q, tinv in zip(probs, tinvs):
            sol = _mm(tinv, jnp.concatenate([_bd(q.pop('vb'), bdm), _bd(q.pop('kbe'), bdm)], axis=1))
            q['u'] = sol[:, 0:PAIR]
            q['wq'] = jnp.concatenate([sol[:, PAIR:], q.pop('qd')], axis=0).astype(BF16)
        yield
        staged[g] = probs

    def chain(g):
        probs = staged.pop(g)
        for j in range(0, len(probs), N_PAIRS):
            cp = probs[j:j + N_PAIRS]
            ss = [s_ref[q['p']] for q in cp]
            wss = [jnp.dot(q['wq'], s.astype(BF16), preferred_element_type=F32) for q, s in zip(cp, ss)]
            yield
            ys = []
            for q, s, ws in zip(cp, ss, wss):
                v_new = q['u'] - ws[0:CHUNK]
                ys.append(ws[CHUNK:] + _mm(q['qk'], _bd(v_new, bdm)))
                upd = lax.dot_general(q['kd'], v_new.astype(BF16), TN_DIMS, preferred_element_type=F32)
                s_ref[q['p']] = s * q['gl'] + jnp.where(same_head, upd, 0.0)
            c = cp[0]['c']
            y_s[c * CHUNK:(c + 1) * CHUNK, :] = jnp.concatenate(ys, axis=1)
            yield

    _software_pipeline(len(groups), phase_a, chain)

    if not final:
        o_ref[...] = y_s[...]
    else:
        o = of_ref[...] + y_s[...]
        ms = _dot_rhs01(o * o, seg) * (1.0 / HEAD_DIM)
        o_ref[...] = (o * lax.rsqrt(ms + 1e-6) * ng_ref[...] * _silu(z_ref[...])).astype(o_ref.dtype)


def _expand_mat(first_lane):
    r = jnp.arange(LANES)[:, None]
    c = jnp.arange(B_W)[None, :] // HEAD_DIM
    return (r == c + first_lane).astype(BF16)


def gdn_pass(pb, p, fwd_out, *, reverse, tb):
    bsz, t_len, _ = pb.shape
    nt = t_len // tb
    final = fwd_out is not None
    d = 1 if reverse else 0
    const2 = lambda a: pl.BlockSpec(a.shape, lambda b, t: (0,) * a.ndim)
    pad_row = lambda a: jnp.zeros((1, LANES), F32).at[0, :a.size].set(a.reshape(-1))
    consts = [pad_row(p['A_log']), pad_row(p['dt_bias']), _seg_mats(),
              _tri_const(reverse), _tri_const(not reverse, reps=2), _expand_mat(d * B_HEADS),
              _expand_mat(2 * B_HEADS + d * B_HEADS)]
    pos = (lambda t: nt - 1 - t) if reverse else (lambda t: t)
    blk_spec = lambda width, col_block: pl.BlockSpec((None, tb, width), lambda b, t: (b, pos(t), col_block))
    io_spec = blk_spec(B_W, 0)
    blk = lambda: pltpu.VMEM((tb, B_W), F32)
    scratch = [pltpu.VMEM((N_PAIRS, PAIR, PAIR), F32), blk(), blk(), blk(), pltpu.VMEM((tb, LANES), F32),
               blk(), blk()]
    if final:
        o_fwd, qkv = fwd_out
        ng = jnp.tile(p['norm_g'].astype(F32), B_HEADS).reshape(1, B_W)
        args = [qkv, pb, pb] + consts + [ng, o_fwd]
        in_specs = ([blk_spec(3 * B_W, 0), blk_spec(B_W, 3), blk_spec(LANES, 4 * B_W // LANES)]
                    + [const2(a) for a in consts] + [const2(ng), io_spec])
        out_shape = jax.ShapeDtypeStruct((bsz, t_len, B_W), BF16)
        out_specs = io_spec
    else:
        main, prev, nxt = _seq_specs(tb, nt, B_IN_PAD, reverse)
        consts = [p['conv_w'].astype(F32)] + consts
        args = [pb, pb, pb] + consts
        in_specs = [main, prev, nxt] + [const2(a) for a in consts]
        out_shape = [jax.ShapeDtypeStruct((bsz, t_len, B_W), F32), jax.ShapeDtypeStruct((bsz, t_len, 3 * B_W), BF16)]
        out_specs = [io_spec, blk_spec(3 * B_W, 0)]
        scratch = [pltpu.VMEM((tb + 2 * HALO, 3 * B_W), F32)] + scratch
    return pl.pallas_call(
        functools.partial(_gdn_kernel, reverse=reverse, final=final, tb=tb, nt=nt, group=CHUNK_GROUP),
        out_shape=out_shape,
        grid=(bsz, nt),
        in_specs=in_specs,
        out_specs=out_specs,
        scratch_shapes=scratch,
        compiler_params=pltpu.CompilerParams(dimension_semantics=("parallel", "arbitrary"),
                                             vmem_limit_bytes=VMEM_LIMIT),
        name="gdn_bwd_final" if final else "gdn_fwd",
    )(*args)


def gdn_mix(pb, p, tb):
    fwd_out = gdn_pass(pb, p, None, reverse=False, tb=tb)
    return gdn_pass(pb, p, fwd_out, reverse=True, tb=tb)


def _mlstm_kernel(*refs, reverse, final, tb, nt):
    if final:
        (qk_ref, v_ref, og_ref, gt_ref, bi_ref, bf_ref, tri_ref, ng_ref, hf_ref, o_ref,
         c_ref, m_ref, q_s, k_s, li_s, lf_s, y_s) = refs
    else:
        (x_ref, xp_ref, xn_ref, cw_ref, bi_ref, bf_ref, tri_ref, o_ref, qk_out_ref,
         xe_ref, c_ref, m_ref, q_s, k_s, li_s, lf_s, y_s) = refs
    d = 1 if reverse else 0
    t = pl.program_id(1)
    pos = (nt - 1 - t) if reverse else t
    nc = tb // CHUNK
    qk_w = 2 * C_W
    dh = C_HEAD_DIM

    @pl.when(t == 0)
    def _():
        c_ref[...] = jnp.zeros(c_ref.shape, F32)
        m_ref[...] = jnp.zeros(m_ref.shape, F32)

    if final:
        q_s[...] = qk_ref[:, 0:C_W].astype(F32)
        k_s[...] = qk_ref[:, C_W:].astype(F32)
        gates = gt_ref[...]
        v_src, v_col = v_ref, 0
    else:
        _fill_extended(xe_ref, x_ref, xp_ref, xn_ref, pos, nt, tb, width=qk_w)
        conv = xe_ref[pl.ds(HALO - 2, tb), :] * cw_ref[0:1, :]
        for j in range(1, CONV_W):
            conv = conv + xe_ref[pl.ds(HALO - 2 + j, tb), :] * cw_ref[j:j + 1, :]
        qk = _silu(conv)
        qs = qk[:, 0:C_W] * (dh ** -0.5)
        q_s[...] = qs
        k_s[...] = qk[:, C_W:]
        qk_out_ref[...] = jnp.concatenate([qs, qk[:, C_W:]], axis=1).astype(BF16)
        gates = x_ref[:, 4 * C_W:4 * C_W + 2 * LANES]
        v_src, v_col = x_ref, 2 * C_W
    li_s[...] = gates[:, 0:LANES] + bi_ref[...]
    lf_s[...] = -_softplus(-(gates[:, LANES:] + bf_ref[...]))

    row = lax.broadcasted_iota(jnp.int32, (CHUNK, CHUNK), 0)
    col = lax.broadcasted_iota(jnp.int32, (CHUNK, CHUNK), 1)
    incl = (col >= row) if reverse else (col <= row)
    eye_b = (col == row).astype(BF16)
    tri = tri_ref[...]

    row128 = lax.broadcasted_iota(jnp.int32, (CHUNK, LANES), 0)

    def running_max(x):
        s = 1
        while s < CHUNK:
            if reverse:
                shifted = jnp.where(row128 >= CHUNK - s, -jnp.inf, pltpu.roll(x, CHUNK - s, axis=0))
            else:
                shifted = jnp.where(row128 < s, -jnp.inf, pltpu.roll(x, s, axis=0))
            x = jnp.maximum(x, shifted)
            s *= 2
        return x

    state = dict(c=[c_ref[h] for h in range(C_HEADS)], m=m_ref[...])
    ones_b = jnp.ones((CHUNK, dh), BF16)

    def gate_pre(c):
        rows = slice(c * CHUNK, (c + 1) * CHUNK)
        bcum = _dot_exact_lhs(tri, lf_s[rows, :])
        gkey = li_s[rows, :] - bcum
        return bcum, gkey, _dot_tn_exact_rhs(gkey, eye_b), bcum + running_max(gkey)

    pre = {c: gate_pre(c) for c in range(nc)}

    def chunk_steps(c):
        rows = slice(c * CHUNK, (c + 1) * CHUNK)
        bcum, gkey, gkey_t, intra_max = pre.pop(c)
        last = bcum[0:1, :] if reverse else bcum[CHUNK - 1:CHUNK, :]
        m = state['m']
        w_end = last + gkey
        m_new = jnp.maximum(last + m, jnp.max(w_end, axis=0, keepdims=True))
        s_old = jnp.exp(last + m - m_new)
        sk_scale = jnp.exp(w_end - m_new)
        a_inter = bcum + m
        state['m'] = m_new
        m_row = jnp.maximum(a_inter, intra_max)
        r_row = bcum - m_row
        s_inter = jnp.exp(a_inter - m_row)
        e_mrow = jnp.exp(-m_row)
        qc = q_s[rows, :]
        kc = k_s[rows, :]
        vc = v_src[rows, v_col:v_col + C_W]
        hd = []
        for h in range(C_HEADS):
            sl = slice(h * dh, (h + 1) * dh)
            qh, kh = qc[:, sl].astype(BF16), kc[:, sl]
            hd.append(dict(h=h, l=d * C_HEADS + h, qh=qh, kh=kh,
                           va=jnp.concatenate([vc[:, sl].astype(BF16), ones_b], axis=1),
                           qk=lax.dot_general(qh, kh.astype(BF16), NT_DIMS, preferred_element_type=F32)))
        for e in hd:
            e['qcn'] = jnp.dot(e.pop('qh'), state['c'][e['h']].astype(BF16), preferred_element_type=F32)
        for e in hd:
            h, l = e['h'], e['l']
            s_k = sk_scale[:, l:l + 1] * e.pop('kh')
            upd = lax.dot_general(s_k.astype(BF16), e['va'], TN_DIMS, preferred_element_type=F32)
            state['c'][h] = s_old[:, l:l + 1] * state['c'][h] + upd
        yield
        ys = []
        for e in hd:
            l = e['l']
            expo = jnp.where(incl, r_row[:, l:l + 1] + gkey_t[l:l + 1, :], -jnp.inf)
            pva = _mm(e['qk'] * jnp.exp(expo), e['va'])
            qcn = e['qcn']
            num = s_inter[:, l:l + 1] * qcn[:, 0:dh] + pva[:, 0:dh]
            den = s_inter[:, l:l + 1] * qcn[:, dh:] + pva[:, dh:]
            ys.append(num / jnp.maximum(jnp.abs(den), e_mrow[:, l:l + 1]))
        y_s[rows, :] = jnp.concatenate(ys, axis=1)
        yield

    order = [(nc - 1 - i) if reverse else i for i in range(nc)]
    prev = None
    for c in order:
        cur = chunk_steps(c)
        next(cur)
        if prev is not None:
            next(prev)
        prev = cur
    next(prev)
    for h in range(C_HEADS):
        c_ref[h] = state['c'][h]
    m_ref[...] = state['m']

    if not final:
        o_ref[...] = y_s[...]
    else:
        hs = hf_ref[...] + y_s[...]
        outs = []
        ones_sq = jnp.ones((dh, dh), BF16)
        for h in range(C_HEADS):
            sl = slice(h * dh, (h + 1) * dh)
            hh = hs[:, sl]
            outs.append(hh * lax.rsqrt(_mm(hh * hh, ones_sq) * (1.0 / dh) + 1e-6))
        hn = jnp.concatenate(outs, axis=1) * ng_ref[...]
        o_ref[...] = (_sigmoid(og_ref[...]) * hn).astype(o_ref.dtype)


def mlstm_pass(pc, p, fwd_out, *, reverse, tb):
    bsz, t_len, _ = pc.shape
    nt = t_len // tb
    final = fwd_out is not None
    const2 = lambda a: pl.BlockSpec(a.shape, lambda b, t: (0,) * a.ndim)
    gate_b = p['gate_b'].astype(F32)
    pad_row = lambda a: jnp.zeros((1, LANES), F32).at[0, :a.size].set(a)
    consts = [pad_row(gate_b[:2 * C_HEADS]), pad_row(gate_b[2 * C_HEADS:]), _tri_const(reverse)]
    pos = (lambda t: nt - 1 - t) if reverse else (lambda t: t)
    blk_spec = lambda width, col_block: pl.BlockSpec((None, tb, width), lambda b, t: (b, pos(t), col_block))
    io_spec = blk_spec(C_W, 0)
    blk = lambda: pltpu.VMEM((tb, C_W), F32)
    gate_blk = lambda: pltpu.VMEM((tb, LANES), F32)
    scratch = [pltpu.VMEM((C_HEADS, C_HEAD_DIM, 2 * C_HEAD_DIM), F32),
               pltpu.VMEM((1, LANES), F32), blk(), blk(), gate_blk(), gate_blk(), blk()]
    if final:
        h_fwd, qk = fwd_out
        ng = p['norm_g'].astype(F32).reshape(1, C_W)
        args = [qk, pc, pc, pc] + consts + [ng, h_fwd]
        in_specs = ([blk_spec(2 * C_W, 0), blk_spec(C_W, 2), blk_spec(C_W, 3), blk_spec(2 * LANES, 2 * C_W // LANES)]
                    + [const2(a) for a in consts] + [const2(ng), io_spec])
        out_shape = jax.ShapeDtypeStruct((bsz, t_len, C_W), BF16)
        out_specs = io_spec
    else:
        main, prev, nxt = _seq_specs(tb, nt, C_IN_PAD, reverse)
        consts = [p['conv_w'].astype(F32)] + consts
        args = [pc, pc, pc] + consts
        in_specs = [main, prev, nxt] + [const2(a) for a in consts]
        out_shape = [jax.ShapeDtypeStruct((bsz, t_len, C_W), F32), jax.ShapeDtypeStruct((bsz, t_len, 2 * C_W), BF16)]
        out_specs = [io_spec, blk_spec(2 * C_W, 0)]
        scratch = [pltpu.VMEM((tb + 2 * HALO, 2 * C_W), F32)] + scratch
    return pl.pallas_call(
        functools.partial(_mlstm_kernel, reverse=reverse, final=final, tb=tb, nt=nt),
        out_shape=out_shape,
        grid=(bsz, nt),
        in_specs=in_specs,
        out_specs=out_specs,
        scratch_shapes=scratch,
        compiler_params=pltpu.CompilerParams(dimension_semantics=("parallel", "arbitrary"),
                                             vmem_limit_bytes=VMEM_LIMIT),
        name="mlstm_bwd_final" if final else "mlstm_fwd",
    )(*args)


def mlstm_mix(pc, p, tb):
    fwd_out = mlstm_pass(pc, p, None, reverse=False, tb=tb)
    return mlstm_pass(pc, p, fwd_out, reverse=True, tb=tb)


def _pad_cols(w, width):
    return jnp.pad(w, ((0, 0), (0, width - w.shape[1])))


def encoder_trunk(x, p, tb=512):
    bsz, t_len, d = x.shape
    n = bsz * t_len
    x2 = x.reshape(n, d)

    w_in = jnp.concatenate([p['even_w_in'][:, :A_IN], _pad_cols(p['even_w_in'][:, A_IN:], B_IN_PAD)], axis=1)
    pa, pb = norm_matmul(x2, p['norm_mix_g'][0], w_in.astype(BF16), (A_IN, B_IN_PAD))
    ya = rwkv_mix(pa.reshape(bsz, t_len, A_IN), p['rwkv'], tb)
    yb = gdn_mix(pb.reshape(bsz, t_len, B_IN_PAD), p['gdn'], tb)
    w_out = p['even_w_out'].astype(BF16)
    x2 = matmul_residual([ya.reshape(n, A_W), yb.reshape(n, B_W)], [w_out[:A_W], w_out[A_W:]], x2)
    x2 = ffn_residual(x2, p['norm_ffn_g'][0], p['ffn_w1'].astype(BF16), p['ffn_w3'].astype(BF16),
                      p['ffn_w2'].astype(BF16))

    w_odd = p['odd_w_in']
    n_gate = 2 * C_HEADS
    w_in = jnp.concatenate([w_odd[:, :4 * C_W], _pad_cols(w_odd[:, 4 * C_W:4 * C_W + n_gate], LANES),
                            _pad_cols(w_odd[:, 4 * C_W + n_gate:], LANES)], axis=1)
    (pc,) = norm_matmul(x2, p['norm_mix_g'][1], w_in.astype(BF16), (C_IN_PAD,))
    yc = mlstm_mix(pc.reshape(bsz, t_len, C_IN_PAD), p['mlstm'], tb)
    x2 = matmul_residual([yc.reshape(n, C_W)], [p['odd_w_out'].astype(BF16)], x2)
    x2 = moe_routed_final(x2, p['norm_ffn_g'][1], p['moe_router_w'], p['moe_router_b'],
                          p['moe_w1'].astype(BF16), p['moe_w3'].astype(BF16), p['moe_w2'].astype(BF16),
                          p['norm_final_g'])
    return x2.reshape(bsz, t_len, d)


def kernel(x_prompt, x_sample, norm_mix_g, norm_ffn_g, norm_final_g, even_w_in, rwkv_shift_mu, rwkv_w0, rwkv_w_up, rwkv_a0, rwkv_a_up, rwkv_g_up, rwkv_k_k, rwkv_k_a, rwkv_r_k, rwkv_ln_g, rwkv_ln_b, gdn_conv_w, gdn_A_log, gdn_dt_bias, gdn_norm_g, even_w_out, odd_w_in, mlstm_conv_w, mlstm_gate_b, mlstm_norm_g, odd_w_out, ffn_w1, ffn_w3, ffn_w2, moe_router_w, moe_router_b, moe_w1, moe_w3, moe_w2):
    p = dict(
        norm_mix_g=norm_mix_g, norm_ffn_g=norm_ffn_g, norm_final_g=norm_final_g,
        even_w_in=even_w_in[0], even_w_out=even_w_out[0], odd_w_in=odd_w_in[0], odd_w_out=odd_w_out[0],
        ffn_w1=ffn_w1[0], ffn_w3=ffn_w3[0], ffn_w2=ffn_w2[0],
        moe_router_w=moe_router_w[0], moe_router_b=moe_router_b[0],
        moe_w1=moe_w1[0], moe_w3=moe_w3[0], moe_w2=moe_w2[0],
        rwkv=dict(shift_mu=rwkv_shift_mu[0], w0=rwkv_w0[0], w_up=rwkv_w_up[0], a0=rwkv_a0[0], a_up=rwkv_a_up[0],
                  g_up=rwkv_g_up[0], k_k=rwkv_k_k[0], k_a=rwkv_k_a[0], r_k=rwkv_r_k[0], ln_g=rwkv_ln_g[0],
                  ln_b=rwkv_ln_b[0]),
        gdn=dict(conv_w=gdn_conv_w[0], A_log=gdn_A_log[0], dt_bias=gdn_dt_bias[0], norm_g=gdn_norm_g[0]),
        mlstm=dict(conv_w=mlstm_conv_w[0], gate_b=mlstm_gate_b[0], norm_g=mlstm_norm_g[0]),
    )
    return (encoder_trunk(x_prompt, p), encoder_trunk(x_sample, p))
```
